```python
import jax, jax.numpy as jnp
from jax import lax
import numpy as np

D_MODEL = 2048
BATCH = 2
SEQ = 4096
DEPTH = 4
DEC_BATCH = 8
DEC_SEQ = 4
PAST_LEN = 16384
PAGE_SIZE = 128

BRANCH_WIDTH = D_MODEL // 2
N_BRANCH = 3
SB_HEAD_DIM = 128
SB_HEADS = BRANCH_WIDTH // SB_HEAD_DIM
SB_WIDTH = SB_HEADS * SB_HEAD_DIM
Q_BLOCK = 128
SB_BIAS_INIT = -8.0
CHUNK = 128
CM_WIDTH = BRANCH_WIDTH
CM_GROUPS = 8
CM_GROUP_DIM = CM_WIDTH // CM_GROUPS
LRU_WIDTH = BRANCH_WIDTH
LRU_BLOCKS = 8
LRU_BLOCK_DIM = LRU_WIDTH // LRU_BLOCKS
CONV_W = 4
LRU_C = 8.0
N_GROUPS = 4
EXPERTS_PER_GROUP = 4
N_EXPERTS = N_GROUPS * EXPERTS_PER_GROUP
TOP_K_IN_GROUP = 2
D_EXPERT = 512
N_MODS = 6
EPS = 1e-6
IN_SPLITS = (SB_WIDTH, 2 * SB_WIDTH, 3 * SB_WIDTH,
             3 * SB_WIDTH + CM_WIDTH, 3 * SB_WIDTH + 2 * CM_WIDTH,
             3 * SB_WIDTH + 2 * CM_WIDTH + LRU_WIDTH, 3 * SB_WIDTH + 2 * CM_WIDTH + 2 * LRU_WIDTH)
IN_COLS = 3 * SB_WIDTH + 2 * CM_WIDTH + 2 * LRU_WIDTH + N_BRANCH * D_MODEL

kernel_name = "hybrid_sb_gmlp_rglru_hmoe_step"


def rmsnorm(x, g):
    xf = x.astype(jnp.float32)
    y = xf * lax.rsqrt(jnp.mean(xf * xf, axis=-1, keepdims=True) + EPS)
    return (y * g.astype(jnp.float32)).astype(x.dtype)


def layernorm(x, g):
    xf = x.astype(jnp.float32)
    mu = jnp.mean(xf, axis=-1, keepdims=True)
    var = jnp.mean(jnp.square(xf - mu), axis=-1, keepdims=True)
    return ((xf - mu) * lax.rsqrt(var + EPS) * g.astype(jnp.float32)).astype(x.dtype)


def sb_weights(z, mask):
    log_beta = jax.nn.log_sigmoid(z)
    log_keep = jnp.where(mask, jax.nn.log_sigmoid(-z), 0.0)
    after = lax.cumsum(log_keep, axis=z.ndim - 1, reverse=True) - log_keep
    return jnp.where(mask, jnp.exp(log_beta + after), 0.0)


def sb_attention_prompt(q, k, v, bias):
    bsz, t, h, dh = q.shape
    nb = t // Q_BLOCK
    scale = dh ** -0.5
    qb = jnp.swapaxes(q.reshape(bsz, nb, Q_BLOCK, h, dh), 0, 1)
    kf = k.astype(jnp.float32)
    k_pos = jnp.arange(t)
    bf = bias.astype(jnp.float32)[None, :, None, None]

    def one_block(args):
        qi, bi = args
        z = jnp.einsum('bqhd,bkhd->bhqk', qi.astype(jnp.float32), kf) * scale + bf
        q_pos = bi * Q_BLOCK + jnp.arange(Q_BLOCK)
        a = sb_weights(z, k_pos[None, :] < q_pos[:, None])
        return jnp.einsum('bhqk,bkhd->bqhd', a.astype(v.dtype), v)

    out = lax.map(one_block, (qb, jnp.arange(nb)))
    return jnp.swapaxes(out, 0, 1).reshape(bsz, t, h, dh)


def sb_attention_sample(q, k_new, v_new, k_past, v_past, bias):
    s = q.shape[1]
    p = k_past.shape[1]
    scale = q.shape[-1] ** -0.5
    qf = q.astype(jnp.float32)
    z = jnp.concatenate([jnp.einsum('bqhd,bkhd->bhqk', qf, k_past.astype(jnp.float32)),
                         jnp.einsum('bqhd,bkhd->bhqk', qf, k_new.astype(jnp.float32))], axis=-1) * scale
    z = z + bias.astype(jnp.float32)[None, :, None, None]
    mask = jnp.concatenate([jnp.ones((s, p), bool), jnp.tril(jnp.ones((s, s), bool), k=-1)], axis=-1)
    a = sb_weights(z, mask).astype(v_new.dtype)
    return (jnp.einsum('bhqk,bkhd->bqhd', a[..., :p], v_past)
            + jnp.einsum('bhqk,bkhd->bqhd', a[..., p:], v_new))


def gather_pages(pool, page_table):
    pages = pool[page_table]
    b, n, ps = pages.shape[:3]
    return pages.reshape((b, n * ps) + pages.shape[3:])


def chunk_mix(u, v, g_norm, w_s, b_s):
    bsz, t, _ = v.shape
    l = min(t, CHUNK)
    vn = layernorm(v, g_norm)
    vg = vn.reshape(bsz, t // l, l, CM_GROUPS, CM_GROUP_DIM)
    w = jnp.tril(w_s[:, :l, :l])
    mixed = jnp.einsum('gts,bnsgc->bntgc', w, vg) + jnp.transpose(b_s[:, :l])[None, None, :, :, None]
    return u * mixed.reshape(bsz, t, CM_WIDTH), vn


def rglru(x, gate, conv_hist, h0, conv_w, conv_b, w_a, b_a, w_x, b_x, lam):
    bsz, t, _ = x.shape
    xp = jnp.concatenate([conv_hist.astype(x.dtype), x], axis=1)
    xc = conv_b + xp[:, 0:t] * conv_w[0]
    for j in range(1, CONV_W):
        xc = xc + xp[:, j:j + t] * conv_w[j]
    xb = xc.reshape(bsz, t, LRU_BLOCKS, LRU_BLOCK_DIM)
    r = jax.nn.sigmoid(jnp.einsum('btnd,nde->btne', xb, w_a).reshape(bsz, t, LRU_WIDTH) + b_a)
    i = jax.nn.sigmoid(jnp.einsum('btnd,nde->btne', xb, w_x).reshape(bsz, t, LRU_WIDTH) + b_x)
    log_a = LRU_C * r.astype(jnp.float32) * jax.nn.log_sigmoid(lam.astype(jnp.float32))
    a = jnp.exp(log_a)
    b = jnp.sqrt(-jnp.expm1(2.0 * log_a)) * (i * xc).astype(jnp.float32)

    def step(h, ab):
        h = ab[0] * h + ab[1]
        return h, h

    h_last, hs = lax.scan(step, h0.astype(jnp.float32), (jnp.swapaxes(a, 0, 1), jnp.swapaxes(b, 0, 1)))
    y = jnp.swapaxes(hs, 0, 1).astype(x.dtype) * jax.nn.gelu(gate)
    return y, xp[:, t:], h_last


def mixer(h, p, l, attend, conv_hist, h0):
    bsz, t, _ = h.shape
    q, k, v, cm_u, cm_v, lx, lg, gates = jnp.split(h @ p['w_in'][l], IN_SPLITS, axis=-1)
    heads = (bsz, t, SB_HEADS, SB_HEAD_DIM)
    q, k, v = q.reshape(heads), k.reshape(heads), v.reshape(heads)
    y_a = attend(l, q, k, v).reshape(bsz, t, SB_WIDTH)
    y_b, vn = chunk_mix(jax.nn.gelu(cm_u), jax.nn.gelu(cm_v), p['cm_norm_g'][l], p['cm_w_s'][l], p['cm_b_s'][l])
    y_c, conv_new, h_new = rglru(lx, lg, conv_hist, h0, p['lru_conv_w'][l], p['lru_conv_b'][l],
                                 p['lru_w_a'][l], p['lru_b_a'][l], p['lru_w_x'][l], p['lru_b_x'][l],
                                 p['lru_lambda'][l])
    branches = jnp.stack([y_a, y_b, y_c], axis=2)
    proj = jnp.einsum('btnw,nwd->btnd', branches, p['w_branch'][l])
    g = jax.nn.sigmoid(gates).reshape(bsz, t, N_BRANCH, D_MODEL)
    merged = jnp.sum(g * proj, axis=2)
    return merged @ p['w_out'][l], k, v, vn, conv_new, h_new


def hier_moe(h, rg_w, rg_b, re_w, re_b, w_gate, w_up, w_down):
    bsz, t, d = h.shape
    hf = h.reshape(-1, d)
    n = hf.shape[0]
    g_logits = (hf @ rg_w + rg_b).astype(jnp.float32)
    g_idx = jnp.argmax(g_logits, axis=-1)
    g_w = jnp.take_along_axis(jax.nn.softmax(g_logits, axis=-1), g_idx[:, None], axis=-1)
    e_logits = (hf @ re_w + re_b).astype(jnp.float32).reshape(n, N_GROUPS, EXPERTS_PER_GROUP)
    e_in = jnp.take_along_axis(e_logits, g_idx[:, None, None], axis=1)[:, 0]
    top_v, top_i = lax.top_k(e_in, TOP_K_IN_GROUP)
    top_p = jax.nn.softmax(top_v, axis=-1) * g_w
    expert_id = g_idx[:, None] * EXPERTS_PER_GROUP + top_i
    combine = jnp.sum(jax.nn.one_hot(expert_id, N_EXPERTS, dtype=jnp.float32) * top_p[..., None], axis=1)
    hid = jax.nn.silu(jnp.einsum('nd,edf->nef', hf, w_gate)) * jnp.einsum('nd,edf->nef', hf, w_up)
    y = jnp.einsum('nef,efd->nd', hid * combine[..., None].astype(hid.dtype), w_down)
    return y.reshape(bsz, t, d)


def trunk(x, c, p, attend, conv_hist, h0):
    ks, vs, cvs, convs, hs = [], [], [], [], []
    for l in range(DEPTH):
        mods = jax.nn.silu(c) @ p['ada_w'][l] + p['ada_b'][l]
        sh1, sc1, g1, sh2, sc2, g2 = jnp.split(mods[:, None, :], N_MODS, axis=-1)
        h = rmsnorm(x, p['norm_mix_g'][l]) * (1.0 + sc1) + sh1
        y, k, v, vn, conv_new, h_new = mixer(h, p, l, attend, conv_hist[l], h0[l])
        x = x + g1 * y
        h = rmsnorm(x, p['norm_ffn_g'][l]) * (1.0 + sc2) + sh2
        x = x + g2 * hier_moe(h, p['router_group_w'][l], p['router_group_b'][l], p['router_expert_w'][l],
                              p['router_expert_b'][l], p['moe_w_gate'][l], p['moe_w_up'][l], p['moe_w_down'][l])
        ks.append(k); vs.append(v); cvs.append(vn); convs.append(conv_new); hs.append(h_new)
    y = rmsnorm(x, p['norm_final_g'])
    return y, jnp.stack(ks), jnp.stack(vs), jnp.stack(cvs), jnp.stack(convs), jnp.stack(hs)


def setup_inputs(seed: int = 0) -> dict:
    key = jax.random.key(seed)
    k = jax.random.split(key, 40)
    f32 = jnp.float32

    def nrm(kk, shape, scale):
        return jax.random.normal(kk, shape, f32) * scale

    n_pages = PAST_LEN // PAGE_SIZE
    n_used = DEC_BATCH * n_pages
    n_phys = n_used + n_used // 4
    page_table = jax.random.permutation(k[0], n_phys)[:n_used].reshape(DEC_BATCH, n_pages).astype(jnp.int32)
    a0 = jax.random.uniform(k[1], (DEPTH, LRU_WIDTH), f32, minval=0.9, maxval=0.999)
    return {
        'x_prompt': nrm(k[2], (BATCH, SEQ, D_MODEL), 1.0),
        'x_sample': nrm(k[3], (DEC_BATCH, DEC_SEQ, D_MODEL), 1.0),
        'cache_k': nrm(k[4], (DEPTH, n_phys, PAGE_SIZE, SB_HEADS, SB_HEAD_DIM), 1.0),
        'cache_v': nrm(k[5], (DEPTH, n_phys, PAGE_SIZE, SB_HEADS, SB_HEAD_DIM), 1.0),
        'state_conv': nrm(k[6], (DEPTH, DEC_BATCH, CONV_W - 1, LRU_WIDTH), 1.0),
        'state_lru': nrm(k[7], (DEPTH, DEC_BATCH, LRU_WIDTH), 0.5),
        'page_table': page_table,
        'c_prompt': nrm(k[8], (BATCH, D_MODEL), 1.0),
        'c_sample': nrm(k[9], (DEC_BATCH, D_MODEL), 1.0),
        'ada_w': nrm(k[10], (DEPTH, D_MODEL, N_MODS * D_MODEL), 0.5 * D_MODEL ** -0.5),
        'ada_b': nrm(k[11], (DEPTH, N_MODS * D_MODEL), 0.02),
        'norm_mix_g': 1.0 + nrm(k[12], (DEPTH, D_MODEL), 0.02),
        'norm_ffn_g': 1.0 + nrm(k[13], (DEPTH, D_MODEL), 0.02),
        'norm_final_g': 1.0 + nrm(k[14], (D_MODEL,), 0.02),
        'w_in': nrm(k[15], (DEPTH, D_MODEL, IN_COLS), D_MODEL ** -0.5),
        'sb_logit_bias': SB_BIAS_INIT + nrm(k[34], (DEPTH, SB_HEADS), 0.1),
        'cm_norm_g': 1.0 + nrm(k[16], (DEPTH, CM_WIDTH), 0.02),
        'cm_w_s': nrm(k[17], (DEPTH, CM_GROUPS, CHUNK, CHUNK), 0.5 * CHUNK ** -0.5),
        'cm_b_s': 1.0 + nrm(k[18], (DEPTH, CM_GROUPS, CHUNK), 0.02),
        'lru_conv_w': nrm(k[19], (DEPTH, CONV_W, LRU_WIDTH), CONV_W ** -0.5),
        'lru_conv_b': nrm(k[20], (DEPTH, LRU_WIDTH), 0.02),
        'lru_w_a': nrm(k[21], (DEPTH, LRU_BLOCKS, LRU_BLOCK_DIM, LRU_BLOCK_DIM), LRU_BLOCK_DIM ** -0.5),
        'lru_b_a': nrm(k[22], (DEPTH, LRU_WIDTH), 0.02),
        'lru_w_x': nrm(k[23], (DEPTH, LRU_BLOCKS, LRU_BLOCK_DIM, LRU_BLOCK_DIM), LRU_BLOCK_DIM ** -0.5),
        'lru_b_x': nrm(k[24], (DEPTH, LRU_WIDTH), 0.02),
        'lru_lambda': jnp.log(a0) - jnp.log1p(-a0),
        'w_branch': nrm(k[25], (DEPTH, N_BRANCH, BRANCH_WIDTH, D_MODEL), BRANCH_WIDTH ** -0.5),
        'w_out': nrm(k[26], (DEPTH, D_MODEL, D_MODEL), D_MODEL ** -0.5),
        'router_group_w': nrm(k[27], (DEPTH, D_MODEL, N_GROUPS), D_MODEL ** -0.5),
        'router_group_b': nrm(k[28], (DEPTH, N_GROUPS), 0.01),
        'router_expert_w': nrm(k[29], (DEPTH, D_MODEL, N_EXPERTS), D_MODEL ** -0.5),
        'router_expert_b': nrm(k[30], (DEPTH, N_EXPERTS), 0.01),
        'moe_w_gate': nrm(k[31], (DEPTH, N_EXPERTS, D_MODEL, D_EXPERT), D_MODEL ** -0.5),
        'moe_w_up': nrm(k[32], (DEPTH, N_EXPERTS, D_MODEL, D_EXPERT), D_MODEL ** -0.5),
        'moe_w_down': nrm(k[33], (DEPTH, N_EXPERTS, D_EXPERT, D_MODEL), D_EXPERT ** -0.5),
    }


def reference(x_prompt, x_sample, cache_k, cache_v, state_conv, state_lru, page_table, c_prompt, c_sample,
              ada_w, ada_b, norm_mix_g, norm_ffn_g, norm_final_g, w_in, sb_logit_bias, cm_norm_g, cm_w_s, cm_b_s,
              lru_conv_w, lru_conv_b, lru_w_a, lru_b_a, lru_w_x, lru_b_x, lru_lambda,
              w_branch, w_out, router_group_w, router_group_b, router_expert_w, router_expert_b,
              moe_w_gate, moe_w_up, moe_w_down):
    p = dict(ada_w=ada_w, ada_b=ada_b, norm_mix_g=norm_mix_g, norm_ffn_g=norm_ffn_g, norm_final_g=norm_final_g,
             w_in=w_in, cm_norm_g=cm_norm_g, cm_w_s=cm_w_s, cm_b_s=cm_b_s,
             lru_conv_w=lru_conv_w, lru_conv_b=lru_conv_b, lru_w_a=lru_w_a, lru_b_a=lru_b_a,
             lru_w_x=lru_w_x, lru_b_x=lru_b_x, lru_lambda=lru_lambda, w_branch=w_branch, w_out=w_out,
             router_group_w=router_group_w, router_group_b=router_group_b,
             router_expert_w=router_expert_w, router_expert_b=router_expert_b,
             moe_w_gate=moe_w_gate, moe_w_up=moe_w_up, moe_w_down=moe_w_down)

    def attend_prompt(l, q, k, v):
        return sb_attention_prompt(q, k, v, sb_logit_bias[l])

    def attend_sample(l, q, k, v):
        return sb_attention_sample(q, k, v, gather_pages(cache_k[l], page_table),
                                   gather_pages(cache_v[l], page_table), sb_logit_bias[l])

    bp = x_prompt.shape[0]
    zero_conv = jnp.zeros((DEPTH, bp, CONV_W - 1, LRU_WIDTH), x_prompt.dtype)
    zero_h = jnp.zeros((DEPTH, bp, LRU_WIDTH), jnp.float32)
    y_prompt, k_prompt, v_prompt, _, conv_prompt, lru_prompt = trunk(
        x_prompt, c_prompt, p, attend_prompt, zero_conv, zero_h)
    y_sample, k_sample, v_sample, chunkv_sample, conv_sample, lru_sample = trunk(
        x_sample, c_sample, p, attend_sample, state_conv, state_lru)
    return (y_prompt, y_sample, k_prompt, v_prompt, k_sample, v_sample, chunkv_sample,
            conv_prompt, conv_sample, lru_prompt, lru_sample)
```

```python
import functools

import jax
import jax.numpy as jnp
from jax import lax
from jax.experimental import pallas as pl
from jax.experimental.pallas import tpu as pltpu

F32 = jnp.float32
BF16 = jnp.bfloat16

EPS = 1e-6
LRU_C = 8.0
N_MODS = 6
N_GROUPS = 4
EXPERTS_PER_GROUP = 4
N_EXPERTS = N_GROUPS * EXPERTS_PER_GROUP
CONV_W = 4
N_BRANCH = 3
LANE = 128
SUBLANE = 8
VMEM_LIMIT = 56 * 1024 * 1024


def _call(kernel, name, grid, in_specs, out_specs, out_shape, scratch=(), sem=None, n_prefetch=0):
    grid_spec = pltpu.PrefetchScalarGridSpec(
        num_scalar_prefetch=n_prefetch, grid=grid, in_specs=in_specs, out_specs=out_specs,
        scratch_shapes=list(scratch))
    return pl.pallas_call(
        kernel, grid_spec=grid_spec, out_shape=out_shape, name=name,
        compiler_params=pltpu.CompilerParams(dimension_semantics=sem, vmem_limit_bytes=VMEM_LIMIT))


def _bdot(a, b):
    return jnp.dot(a.astype(BF16), b.astype(BF16), preferred_element_type=F32)


def _split_hi_lo(x):
    hi = x.astype(BF16)
    lo = (x - hi.astype(F32)).astype(BF16)
    return hi, lo


def _softplus_neg_abs(z):
    u = jnp.exp(-jnp.abs(z))
    return jnp.where(u < 1e-3, u * (1.0 - u * (0.5 - u * (1.0 / 3.0))), jnp.log(1.0 + u))


def _log_sigmoid_pair(z):
    t = _softplus_neg_abs(z)
    return jnp.minimum(z, 0.0) - t, -jnp.maximum(z, 0.0) - t


def _neg_expm1(y):
    small = -y * (1.0 + y * (0.5 + y * (1.0 / 6.0 + y * (1.0 / 24.0))))
    return jnp.where(y > -0.05, small, 1.0 - jnp.exp(y))


def _silu(x):
    return x * jax.nn.sigmoid(x)


def _gelu(x):
    return jax.nn.gelu(x, approximate=True)


def _ada_kernel(c_ref, w_ref, b_ref, o_ref):
    o_ref[...] = _bdot(_silu(c_ref[...]), w_ref[...]) + b_ref[...]


def _ada_mods(c_all, ada_w, ada_b):
    n_l, d, n = ada_w.shape
    r = c_all.shape[0]
    tn = 1024
    return _call(
        _ada_kernel, "ada_mods", (n_l, n // tn),
        [pl.BlockSpec((r, d), lambda l, j: (0, 0)),
         pl.BlockSpec((None, d, tn), lambda l, j: (l, 0, j)),
         pl.BlockSpec((None, 1, tn), lambda l, j: (l, 0, j))],
        pl.BlockSpec((None, r, tn), lambda l, j: (l, 0, j)),
        jax.ShapeDtypeStruct((n_l, r, n), F32), sem=("parallel", "parallel"),
    )(c_all, ada_w, ada_b.reshape(n_l, 1, n))


class _Group:
    def __init__(self, mods, rows, tm, rows_per_batch, per_token):
        self.mods, self.rows, self.tm = mods, rows, tm
        self.rows_per_batch, self.per_token = rows_per_batch, per_token

    def mod_spec(self, l, k, d, tn=None, tm=None):
        tn_ = d if tn is None else tn
        tm_ = self.tm if tm is None else tm
        nj = d // tn_

        def col(j):
            return k * nj + (0 if tn is None else j)

        if self.per_token:
            return pl.BlockSpec((None, tm_, tn_), lambda i, j: (l, i, col(j)))
        tpb = self.rows_per_batch // tm_
        return pl.BlockSpec((None, None, 1, tn_), lambda i, j: (l, i // tpb, 0, col(j)))


def _modulated_norm(x, g, sc, sh):
    y = x * lax.rsqrt(jnp.mean(x * x, axis=-1, keepdims=True) + EPS) * g
    return y * (1.0 + sc) + sh


def _in_proj_kernel(x_ref, g_ref, sh_ref, sc_ref, w_ref, o_ref, h_ref):
    @pl.when(pl.program_id(1) == 0)
    def _():
        h_ref[...] = _modulated_norm(x_ref[...], g_ref[...], sc_ref[...], sh_ref[...]).astype(BF16)

    o_ref[...] = _bdot(h_ref[...], w_ref[...])


def _in_proj(grp, l, x, g, w_in):
    rows, d = x.shape
    n = w_in.shape[2]
    tm, tn = grp.tm, 512
    return _call(
        _in_proj_kernel, "in_proj", (rows // tm, n // tn),
        [pl.BlockSpec((tm, d), lambda i, j: (i, 0)),
         pl.BlockSpec((None, 1, d), lambda i, j: (l, 0, 0)),
         grp.mod_spec(l, 0, d), grp.mod_spec(l, 1, d),
         pl.BlockSpec((None, d, tn), lambda i, j: (l, 0, j))],
        pl.BlockSpec((tm, tn), lambda i, j: (i, j)),
        jax.ShapeDtypeStruct((rows, n), F32),
        scratch=[pltpu.VMEM((tm, d), BF16)], sem=("parallel", "arbitrary"),
    )(x, g, grp.mods, grp.mods, w_in)


def _cumsum_matrix(n):
    j = jnp.arange(n)[:, None]
    s = jnp.arange(n)[None, :]
    u = (j > s).astype(BF16)
    blk = jnp.concatenate([u, jnp.ones((n, n), BF16)], axis=1)
    return jnp.concatenate([blk, blk], axis=0)


def _sb_prompt_kernel(bias_ref, q_ref, k_ref, v_ref, uu_ref, o_ref, c_ref, acc_ref, *, tq, ck, scale):
    h = pl.program_id(1)
    qi = pl.program_id(2)
    bias = bias_ref[h]
    q = q_ref[...].astype(BF16)
    acc_ref[...] = jnp.zeros_like(acc_ref)
    c_ref[...] = jnp.zeros_like(c_ref)
    n_chunks = tq // ck

    def key_tile(kb, masked):
        for ci in reversed(range(n_chunks)):
            start = pl.multiple_of(kb * tq + ci * ck, ck)
            kc = k_ref[pl.ds(start, ck), :]
            vc = v_ref[pl.ds(start, ck), :]
            z = lax.dot_general(q, kc.astype(BF16), (((1,), (1,)), ((), ())),
                                preferred_element_type=F32) * scale + bias
            log_beta, log_keep = _log_sigmoid_pair(z)
            if masked:
                t_pos = lax.broadcasted_iota(jnp.int32, (tq, ck), 0)
                s_pos = lax.broadcasted_iota(jnp.int32, (tq, ck), 1) + ci * ck
                mask = s_pos < t_pos
                log_keep = jnp.where(mask, log_keep, 0.0)
            hi, lo = _split_hi_lo(log_keep)
            cs = jnp.dot(jnp.concatenate([hi, lo], axis=1), uu_ref[...], preferred_element_type=F32)
            c = c_ref[...]
            p = jnp.exp(log_beta + cs[:, :ck] + c)
            if masked:
                p = jnp.where(mask, p, 0.0)
            acc_ref[...] += _bdot(p, vc)
            c_ref[...] = c + cs[:, ck:]

    key_tile(qi, True)

    def body(i, carry):
        key_tile(qi - 1 - i, False)
        return carry

    lax.fori_loop(0, qi, body, 0)
    o_ref[...] = acc_ref[...].astype(o_ref.dtype)


def _sb_attention_prompt(proj, bias, n_batch, t, n_heads, dh):
    tq = min(512, t)
    ck = LANE
    nq = t // tq
    kernel = functools.partial(_sb_prompt_kernel, tq=tq, ck=ck, scale=dh ** -0.5)
    return _call(
        kernel, "sb_prompt", (n_batch, n_heads, nq),
        [pl.BlockSpec(memory_space=pltpu.SMEM),
         pl.BlockSpec((tq, dh), lambda b, h, i: (b * nq + i, h)),
         pl.BlockSpec((t, dh), lambda b, h, i: (b, n_heads + h)),
         pl.BlockSpec((t, dh), lambda b, h, i: (b, 2 * n_heads + h)),
         pl.BlockSpec((2 * ck, 2 * ck), lambda b, h, i: (0, 0))],
        pl.BlockSpec((tq, dh), lambda b, h, i: (b * nq + i, h)),
        jax.ShapeDtypeStruct((n_batch * t, n_heads * dh), BF16),
        scratch=[pltpu.VMEM((tq, ck), F32), pltpu.VMEM((tq, dh), F32)],
        sem=("parallel", "parallel", "arbitrary"),
    )(bias, proj, proj, proj, _cumsum_matrix(ck))


def _sb_sample_kernel(pt_ref, qbd_ref, bias_ref, kn_ref, vn_ref, ut_ref, *rest, n_pp, s_new, n_q, nhq, scale):
    k_refs = rest[:n_pp]
    v_refs = rest[n_pp:2 * n_pp]
    o_ref, c_ref, acc_ref = rest[2 * n_pp:]
    j = pl.program_id(1)
    n_heads, dh = kn_ref.shape[1], kn_ref.shape[2]
    bias = bias_ref[...]

    def visit(k_ref, v_ref, masked):
        z = None
        for h in range(n_heads):
            zh = _bdot(k_ref[:, h, :], qbd_ref[h * dh:(h + 1) * dh, :])
            z = zh if z is None else z + zh
        z = z * scale + bias
        log_beta, log_keep = _log_sigmoid_pair(z)
        if masked:
            s_pos = lax.broadcasted_iota(jnp.int32, z.shape, 0)
            q_pos = lax.broadcasted_iota(jnp.int32, z.shape, 1) % n_q
            mask = (s_pos < q_pos) & (s_pos < s_new)
            log_keep = jnp.where(mask, log_keep, 0.0)
        hi, lo = _split_hi_lo(log_keep)
        cs = jnp.dot(ut_ref[...], jnp.concatenate([hi, lo], axis=0), preferred_element_type=F32)
        c = c_ref[...]
        p = jnp.exp(log_beta + cs + c)
        if masked:
            p = jnp.where(mask, p, 0.0)
        c_ref[...] = c + cs[0:1, :] + log_keep[0:1, :]
        pt = jnp.transpose(p)[:nhq, :].astype(BF16)
        for h in range(n_heads):
            acc_ref[:, h * dh:(h + 1) * dh] += _bdot(pt, v_ref[:, h, :])

    @pl.when(j == 0)
    def _():
        c_ref[...] = jnp.zeros_like(c_ref)
        acc_ref[...] = jnp.zeros_like(acc_ref)
        visit(kn_ref, vn_ref, True)

    for i in range(n_pp):
        visit(k_refs[i], v_refs[i], False)

    @pl.when(j == pl.num_programs(1) - 1)
    def _():
        o_ref[...] = acc_ref[...]


def _sb_attention_sample(l, q, k_new, v_new, cache_k, cache_v, page_table, bias):
    n_b, s_new, width = q.shape
    _, n_phys, page, n_heads, dh = cache_k.shape
    n_pages = page_table.shape[1]
    nhq = n_heads * s_new
    n_pp = 4 if n_pages % 4 == 0 else 1
    q4 = q.reshape(n_b, s_new, n_heads, dh)
    qbd = jnp.einsum('bqhd,hg->bhdgq', q4, jnp.eye(n_heads, dtype=q.dtype)).reshape(n_b, width, nhq)
    qbd = jnp.pad(qbd, ((0, 0), (0, 0), (0, LANE - nhq))).astype(BF16)
    bias_l = jnp.pad(jnp.repeat(bias, s_new), (0, LANE - nhq)).reshape(1, LANE)
    kn = jnp.pad(k_new, ((0, 0), (0, page - s_new), (0, 0))).reshape(n_b, page, n_heads, dh)
    vn = jnp.pad(v_new, ((0, 0), (0, page - s_new), (0, 0))).reshape(n_b, page, n_heads, dh)
    jj = jnp.arange(page)[None, :]
    ss = jnp.arange(page)[:, None]
    ut = (jj > ss).astype(BF16)
    ut2 = jnp.concatenate([ut, ut], axis=1)

    def page_spec(i):
        return pl.BlockSpec((None, None, page, n_heads, dh),
                            lambda b, j, pt: (l, pt[b, n_pages - 1 - (j * n_pp + i)], 0, 0, 0))

    kernel = functools.partial(_sb_sample_kernel, n_pp=n_pp, s_new=s_new, n_q=s_new, nhq=nhq, scale=dh ** -0.5)
    out = _call(
        kernel, "sb_sample", (n_b, n_pages // n_pp),
        [pl.BlockSpec((None, width, LANE), lambda b, j, pt: (b, 0, 0)),
         pl.BlockSpec((1, LANE), lambda b, j, pt: (0, 0)),
         pl.BlockSpec((None, page, n_heads, dh), lambda b, j, pt: (b, 0, 0, 0)),
         pl.BlockSpec((None, page, n_heads, dh), lambda b, j, pt: (b, 0, 0, 0)),
         pl.BlockSpec((page, 2 * page), lambda b, j, pt: (0, 0))]
        + [page_spec(i) for i in range(n_pp)] + [page_spec(i) for i in range(n_pp)],
        pl.BlockSpec((None, nhq, width), lambda b, j, pt: (b, 0, 0)),
        jax.ShapeDtypeStruct((n_b, nhq, width), F32),
        scratch=[pltpu.VMEM((1, LANE), F32), pltpu.VMEM((nhq, width), F32)],
        sem=("parallel", "arbitrary"), n_prefetch=1,
    )(page_table, qbd, bias_l, kn, vn, ut2, *([cache_k] * n_pp), *([cache_v] * n_pp))
    o5 = out.reshape(n_b, n_heads, s_new, n_heads, dh)
    hh = jnp.arange(n_heads)
    diag = o5[:, hh, :, hh, :]
    return jnp.transpose(diag, (1, 2, 0, 3)).reshape(n_b, s_new, width)


def _chunk_mix_kernel(u_ref, v_ref, g_ref, w_ref, b_ref, y_ref, *vn_out, lc, period, groups):
    u = _gelu(u_ref[...])
    v = _gelu(v_ref[...])
    mu = jnp.mean(v, axis=-1, keepdims=True)
    var = jnp.mean(jnp.square(v - mu), axis=-1, keepdims=True)
    vn = (v - mu) * lax.rsqrt(var + EPS) * g_ref[...]
    if vn_out:
        vn_out[0][...] = vn
    vb = vn.astype(BF16)
    tm, width = u.shape
    gd = width // groups
    n_ch = tm // lc
    r = lax.broadcasted_iota(jnp.int32, (lc, lc), 0)
    c = lax.broadcasted_iota(jnp.int32, (lc, lc), 1)
    mask = c <= r if period == lc else (r // period == c // period) & (c <= r)
    for g in range(groups):
        w = jnp.where(mask, w_ref[g], 0.0).astype(BF16)
        rhs = jnp.concatenate([vb[ch * lc:(ch + 1) * lc, g * gd:(g + 1) * gd] for ch in range(n_ch)], axis=1)
        mixed = jnp.dot(w, rhs, preferred_element_type=F32) + b_ref[g]
        for ch in range(n_ch):
            rows = slice(ch * lc, (ch + 1) * lc)
            cols = slice(g * gd, (g + 1) * gd)
            y_ref[rows, cols] = (u[rows, cols] * mixed[:, ch * gd:(ch + 1) * gd]).astype(y_ref.dtype)


def _chunk_mix(proj, col_u, col_v, g, w, b, tm, lc, period, emit_vn):
    rows = proj.shape[0]
    groups = w.shape[0]
    width = g.shape[-1]
    kernel = functools.partial(_chunk_mix_kernel, lc=lc, period=period, groups=groups)
    out_specs = [pl.BlockSpec((tm, width), lambda i: (i, 0))]
    out_shape = [jax.ShapeDtypeStruct((rows, width), BF16)]
    if emit_vn:
        out_specs.append(pl.BlockSpec((tm, width), lambda i: (i, 0)))
        out_shape.append(jax.ShapeDtypeStruct((rows, width), F32))
    return _call(
        kernel, "chunk_mix", (rows // tm,),
        [pl.BlockSpec((tm, width), lambda i: (i, col_u)),
         pl.BlockSpec((tm, width), lambda i: (i, col_v)),
         pl.BlockSpec((1, width), lambda i: (0, 0)),
         pl.BlockSpec((groups, lc, lc), lambda i: (0, 0, 0)),
         pl.BlockSpec((groups, lc, 1), lambda i: (0, 0, 0))],
        out_specs, out_shape, sem=("parallel",),
    )(proj, proj, g.reshape(1, width), w, b)


def _shift_rows(x, d, fill, row):
    tt = x.shape[0]
    if d % SUBLANE == 0:
        return jnp.concatenate([jnp.full((d,) + x.shape[1:], fill, x.dtype), x[:tt - d]], axis=0)
    return jnp.where(row < d, fill, pltpu.roll(x, d, 0))


def _rglru_kernel(x_ref, gate_ref, hist_ref, h0_ref, cw_ref, cb_ref, wa_ref, ba_ref, wx_ref, bx_ref, lam_ref,
                  y_ref, tail_ref, hs_ref, hist_s, h_s):
    t = pl.program_id(1)

    @pl.when(t == 0)
    def _():
        hist_s[...] = hist_ref[...]
        h_s[...] = jnp.broadcast_to(h0_ref[...], h_s.shape)

    x = x_ref[...]
    tt, width = x.shape
    row = lax.broadcasted_iota(jnp.int32, (tt, width), 0)
    row8 = lax.broadcasted_iota(jnp.int32, (SUBLANE, width), 0)
    hist = hist_s[...]
    cw = cw_ref[...]
    xc = cb_ref[...] + x * cw[CONV_W - 1:CONV_W, :]
    for s in range(1, CONV_W):
        xr = pltpu.roll(x, s, 0)
        top = jnp.where(row8 < s, pltpu.roll(hist, s, 0), xr[:SUBLANE])
        xs = top if tt == SUBLANE else jnp.concatenate([top, xr[SUBLANE:]], axis=0)
        xc = xc + xs * cw[CONV_W - 1 - s:CONV_W - s, :]

    n_blk = wa_ref.shape[0]
    bd = width // n_blk
    ra, ri = [], []
    for n in range(n_blk):
        xb = xc[:, n * bd:(n + 1) * bd].astype(BF16)
        ra.append(jnp.dot(xb, wa_ref[n].astype(BF16), preferred_element_type=F32))
        ri.append(jnp.dot(xb, wx_ref[n].astype(BF16), preferred_element_type=F32))
    r_gate = jax.nn.sigmoid(jnp.concatenate(ra, axis=1) + ba_ref[...])
    i_gate = jax.nn.sigmoid(jnp.concatenate(ri, axis=1) + bx_ref[...])
    lam = lam_ref[...]
    log_sig_lam = jnp.minimum(lam, 0.0) - _softplus_neg_abs(lam)
    log_a = LRU_C * r_gate * log_sig_lam
    a = jnp.exp(log_a)
    b = jnp.sqrt(_neg_expm1(2.0 * log_a)) * (i_gate * xc)

    d = 1
    while d < tt:
        b = a * _shift_rows(b, d, 0.0, row) + b
        a = a * _shift_rows(a, d, 1.0, row)
        d *= 2
    hs = a * h_s[SUBLANE - 1:SUBLANE, :] + b

    y_ref[...] = (hs * _gelu(gate_ref[...])).astype(y_ref.dtype)
    hist_s[...] = x[tt - SUBLANE:, :]
    h_s[...] = hs[tt - SUBLANE:, :]
    tail_ref[...] = x[tt - SUBLANE:, :]
    hs_ref[...] = hs[tt - SUBLANE:, :]


def _rglru(xsrc, col_x, col_g, n_batch, t, tt, hist8, h0, cw, cb, wa, ba, wx, bx, lam):
    width = lam.shape[-1]
    nt = t // tt
    row2 = lambda a: a.reshape(1, width)
    full = lambda shape: pl.BlockSpec(shape, lambda b, i: (0,) * len(shape))
    return _call(
        _rglru_kernel, "rglru", (n_batch, nt),
        [pl.BlockSpec((tt, width), lambda b, i: (b * nt + i, col_x)),
         pl.BlockSpec((tt, width), lambda b, i: (b * nt + i, col_g)),
         pl.BlockSpec((None, SUBLANE, width), lambda b, i: (b, 0, 0)),
         pl.BlockSpec((None, 1, width), lambda b, i: (b, 0, 0)),
         full((CONV_W, width)), full((1, width)),
         full(wa.shape), full((1, width)), full(wx.shape), full((1, width)), full((1, width))],
        [pl.BlockSpec((tt, width), lambda b, i: (b * nt + i, 0)),
         pl.BlockSpec((None, SUBLANE, width), lambda b, i: (b, 0, 0)),
         pl.BlockSpec((None, SUBLANE, width), lambda b, i: (b, 0, 0))],
        [jax.ShapeDtypeStruct((n_batch * t, width), BF16 if tt % (2 * SUBLANE) == 0 else F32),
         jax.ShapeDtypeStruct((n_batch, SUBLANE, width), F32),
         jax.ShapeDtypeStruct((n_batch, SUBLANE, width), F32)],
        scratch=[pltpu.VMEM((SUBLANE, width), F32), pltpu.VMEM((SUBLANE, width), F32)],
        sem=("parallel", "arbitrary"),
    )(xsrc, xsrc, hist8, h0.reshape(n_batch, 1, width), cw, row2(cb), wa, row2(ba), wx, row2(bx), row2(lam))


def _branch_kernel(ya_ref, yb_ref, yc_ref, ga_ref, gb_ref, gc_ref, w_ref, o_ref):
    acc = None
    for n, (y_ref, g_ref) in enumerate(((ya_ref, ga_ref), (yb_ref, gb_ref), (yc_ref, gc_ref))):
        term = jax.nn.sigmoid(g_ref[...]) * _bdot(y_ref[...], w_ref[n])
        acc = term if acc is None else acc + term
    o_ref[...] = acc.astype(o_ref.dtype)


def _branch_merge(grp, l, ya, yb, yc, proj, gate_col0, w_branch):
    rows, bw = ya.shape
    d = w_branch.shape[-1]
    tm, tn = grp.tm, 512
    y_spec = pl.BlockSpec((tm, bw), lambda i, j: (i, 0))

    def gate_spec(n):
        return pl.BlockSpec((tm, tn), lambda i, j: (i, (gate_col0 + n * d) // tn + j))

    return _call(
        _branch_kernel, "branch_merge", (rows // tm, d // tn),
        [y_spec, y_spec, y_spec, gate_spec(0), gate_spec(1), gate_spec(2),
         pl.BlockSpec((None, N_BRANCH, bw, tn), lambda i, j: (l, 0, 0, j))],
        pl.BlockSpec((tm, tn), lambda i, j: (i, j)),
        jax.ShapeDtypeStruct((rows, d), BF16), sem=("parallel", "arbitrary"),
    )(ya, yb, yc, proj, proj, proj, w_branch)


def _out_proj_kernel(m_ref, w_ref, x_ref, g_ref, o_ref):
    o_ref[...] = x_ref[...] + g_ref[...] * _bdot(m_ref[...], w_ref[...])


def _out_proj(grp, l, merged, w_out, x):
    rows, d = x.shape
    tm, tn = grp.tm, 512
    return _call(
        _out_proj_kernel, "out_proj", (rows // tm, d // tn),
        [pl.BlockSpec((tm, d), lambda i, j: (i, 0)),
         pl.BlockSpec((None, d, tn), lambda i, j: (l, 0, j)),
         pl.BlockSpec((tm, tn), lambda i, j: (i, j)),
         grp.mod_spec(l, 2, d, tn)],
        pl.BlockSpec((tm, tn), lambda i, j: (i, j)),
        jax.ShapeDtypeStruct((rows, d), F32), sem=("parallel", "arbitrary"),
    )(merged, w_out, x, grp.mods)


def _first_index_of_max(vals):
    m = functools.reduce(jnp.maximum, vals)
    idx = jnp.full(m.shape, len(vals) - 1, jnp.int32)
    for k in reversed(range(len(vals) - 1)):
        idx = jnp.where(vals[k] == m, k, idx)
    return m, idx


def _moe_prep_kernel(x_ref, g_ref, sh_ref, sc_ref, rw_ref, rb_ref, h_ref, r_ref):
    h = _modulated_norm(x_ref[...], g_ref[...], sc_ref[...], sh_ref[...])
    h_ref[...] = h.astype(BF16)
    h_hi, h_lo = _split_hi_lo(h)
    w_hi, w_lo = _split_hi_lo(rw_ref[...])
    dot = functools.partial(jnp.dot, preferred_element_type=F32)
    logits = dot(h_hi, w_hi) + dot(h_lo, w_hi) + dot(h_hi, w_lo) + rb_ref[...]
    col = lambda k: logits[:, k:k + 1]
    gl = [col(k) for k in range(N_GROUPS)]
    g_max, g_idx = _first_index_of_max(gl)
    g_w = 1.0 / functools.reduce(jnp.add, [jnp.exp(v - g_max) for v in gl])
    e_in = []
    for j in range(EXPERTS_PER_GROUP):
        v = col(N_GROUPS + (N_GROUPS - 1) * EXPERTS_PER_GROUP + j)
        for g in reversed(range(N_GROUPS - 1)):
            v = jnp.where(g_idx == g, col(N_GROUPS + g * EXPERTS_PER_GROUP + j), v)
        e_in.append(v)
    m1, i1 = _first_index_of_max(e_in)
    rest = [jnp.where(i1 == j, -jnp.inf, e_in[j]) for j in range(EXPERTS_PER_GROUP)]
    m2, i2 = _first_index_of_max(rest)
    ratio = jnp.exp(m2 - m1)
    p1 = g_w / (1.0 + ratio)
    p2 = p1 * ratio
    lane = lax.broadcasted_iota(jnp.int32, r_ref.shape, 1)
    e_grp = lane // EXPERTS_PER_GROUP
    e_loc = lane % EXPERTS_PER_GROUP
    w = jnp.where(e_loc == i1, p1, 0.0) + jnp.where(e_loc == i2, p2, 0.0)
    r_ref[...] = jnp.where(e_grp == g_idx, w, 0.0)


def _moe_prep(grp, l, x, g, rw, rb):
    rows, d = x.shape
    tm = grp.tm
    return _call(
        _moe_prep_kernel, "moe_prep", (rows // tm, 1),
        [pl.BlockSpec((tm, d), lambda i, j: (i, 0)),
         pl.BlockSpec((None, 1, d), lambda i, j: (l, 0, 0)),
         grp.mod_spec(l, 3, d), grp.mod_spec(l, 4, d),
         pl.BlockSpec((d, LANE), lambda i, j: (0, 0)),
         pl.BlockSpec((1, LANE), lambda i, j: (0, 0))],
        [pl.BlockSpec((tm, d), lambda i, j: (i, 0)), pl.BlockSpec((tm, LANE), lambda i, j: (i, 0))],
        [jax.ShapeDtypeStruct((rows, d), BF16), jax.ShapeDtypeStruct((rows, LANE), F32)],
        sem=("parallel", "arbitrary"),
    )(x, g, grp.mods, grp.mods, rw, rb)


def _moe_kernel(h_ref, r_ref, wg_ref, wu_ref, wd_ref, x_ref, g_ref, o_ref):
    e = pl.program_id(1)

    @pl.when(e == 0)
    def _():
        o_ref[...] = jnp.zeros_like(o_ref)

    h = h_ref[...]
    a = _bdot(h, wg_ref[...])
    b = _bdot(h, wu_ref[...])
    lane = lax.broadcasted_iota(jnp.int32, r_ref.shape, 1)
    c = jnp.sum(jnp.where(lane == e, r_ref[...], 0.0), axis=1, keepdims=True)
    o_ref[...] += _bdot(_silu(a) * b * c, wd_ref[...])

    @pl.when(e == pl.num_programs(1) - 1)
    def _():
        o_ref[...] = x_ref[...] + g_ref[...] * o_ref[...]


def _moe(grp, l, h, route, w_gate, w_up, w_down, x):
    rows, d = x.shape
    n_e, _, f = w_gate.shape[1:]
    tm = min(grp.tm, 512)
    return _call(
        _moe_kernel, "moe", (rows // tm, n_e),
        [pl.BlockSpec((tm, d), lambda i, e: (i, 0)),
         pl.BlockSpec((tm, LANE), lambda i, e: (i, 0)),
         pl.BlockSpec((None, None, d, f), lambda i, e: (l, e, 0, 0)),
         pl.BlockSpec((None, None, d, f), lambda i, e: (l, e, 0, 0)),
         pl.BlockSpec((None, None, f, d), lambda i, e: (l, e, 0, 0)),
         pl.BlockSpec((tm, d), lambda i, e: (i, 0)),
         grp.mod_spec(l, 5, d, tm=tm)],
        pl.BlockSpec((tm, d), lambda i, e: (i, 0)),
        jax.ShapeDtypeStruct((rows, d), F32), sem=("parallel", "arbitrary"),
    )(h, route, w_gate, w_up, w_down, x, grp.mods)


def _final_norm_kernel(x_ref, g_ref, o_ref):
    x = x_ref[...]
    o_ref[...] = x * lax.rsqrt(jnp.mean(x * x, axis=-1, keepdims=True) + EPS) * g_ref[...]


def _final_norm(x, g, tm):
    rows, d = x.shape
    return _call(
        _final_norm_kernel, "final_norm", (rows // tm,),
        [pl.BlockSpec((tm, d), lambda i: (i, 0)), pl.BlockSpec((1, d), lambda i: (0, 0))],
        pl.BlockSpec((tm, d), lambda i: (i, 0)),
        jax.ShapeDtypeStruct((rows, d), F32), sem=("parallel",),
    )(x, g.reshape(1, d))


def kernel(x_prompt, x_sample, cache_k, cache_v, state_conv, state_lru, page_table, c_prompt, c_sample, ada_w, ada_b, norm_mix_g, norm_ffn_g, norm_final_g, w_in, sb_logit_bias, cm_norm_g, cm_w_s, cm_b_s, lru_conv_w, lru_conv_b, lru_w_a, lru_b_a, lru_w_x, lru_b_x, lru_lambda, w_branch, w_out, router_group_w, router_group_b, router_expert_w, router_expert_b, moe_w_gate, moe_w_up, moe_w_down):
    n_bp, t_p, d = x_prompt.shape
    n_bs, t_s, _ = x_sample.shape
    depth = w_in.shape[0]
    n_heads, dh = cache_k.shape[3], cache_k.shape[4]
    sbw = n_heads * dh
    cmw = cm_norm_g.shape[-1]
    lw = lru_lambda.shape[-1]
    chunk = cm_w_s.shape[-1]
    assert t_s >= CONV_W - 1 and t_s <= SUBLANE and t_p % chunk == 0
    col_q, col_k, col_v = 0, sbw, 2 * sbw
    col_cu, col_cv = 3 * sbw, 3 * sbw + cmw
    col_lx, col_lg = 3 * sbw + 2 * cmw, 3 * sbw + 2 * cmw + lw
    col_gates = 3 * sbw + 2 * cmw + 2 * lw
    assert cmw == sbw and lw == sbw

    n_c = n_bp + n_bs
    c_rows = -(-n_c // SUBLANE) * SUBLANE
    c_all = jnp.pad(jnp.concatenate([c_prompt, c_sample], axis=0), ((0, c_rows - n_c), (0, 0)))
    mods = _ada_mods(c_all, ada_w, ada_b)
    rows_p, rows_s = n_bp * t_p, n_bs * t_s
    tm_p = min(1024, t_p)
    grp_p = _Group(mods[:, :n_bp].reshape(depth, n_bp, 1, N_MODS * d), rows_p, tm_p, t_p, False)
    grp_s = _Group(jnp.repeat(mods[:, n_bp:n_c], t_s, axis=1), rows_s, rows_s, t_s, True)

    norm_mix = norm_mix_g.reshape(depth, 1, d)
    norm_ffn = norm_ffn_g.reshape(depth, 1, d)
    rw = jnp.pad(jnp.concatenate([router_group_w, router_expert_w], axis=-1),
                 ((0, 0), (0, 0), (0, LANE - N_GROUPS - N_EXPERTS)))
    rb = jnp.pad(jnp.concatenate([router_group_b, router_expert_b], axis=-1),
                 ((0, 0), (0, LANE - N_GROUPS - N_EXPERTS))).reshape(depth, 1, LANE)

    n_rep = rows_s // t_s
    cm_w_small = jnp.tile(cm_w_s[:, :, :t_s, :t_s], (1, 1, n_rep, n_rep))
    cm_b_small = jnp.tile(cm_b_s[:, :, :t_s], (1, 1, n_rep))[..., None]
    cm_b_full = cm_b_s[..., None]
    zero_hist = jnp.zeros((n_bp, SUBLANE, lw), F32)
    zero_h = jnp.zeros((n_bp, lw), F32)
    hist_s = jnp.pad(state_conv, ((0, 0), (0, 0), (SUBLANE - (CONV_W - 1), 0), (0, 0)))
    tt_p = min(256, t_p)
    tm_cm = min(512, t_p)

    def layer(l, grp, x, prompt):
        proj = _in_proj(grp, l, x, norm_mix, w_in)
        blk = lambda c: c // sbw
        if prompt:
            k_new = proj[:, col_k:col_k + sbw].reshape(n_bp, t_p, n_heads, dh)
            v_new = proj[:, col_v:col_v + sbw].reshape(n_bp, t_p, n_heads, dh)
            y_a = _sb_attention_prompt(proj, sb_logit_bias[l], n_bp, t_p, n_heads, dh)
            y_b, = _chunk_mix(proj, blk(col_cu), blk(col_cv), cm_norm_g[l], cm_w_s[l], cm_b_full[l],
                              tm_cm, chunk, chunk, False)
            vn = None
            y_c, tail, hs = _rglru(proj, blk(col_lx), blk(col_lg), n_bp, t_p, tt_p, zero_hist, zero_h,
                                   lru_conv_w[l], lru_conv_b[l], lru_w_a[l], lru_b_a[l], lru_w_x[l], lru_b_x[l],
                                   lru_lambda[l])
            t_last = t_p
        else:
            q3 = proj[:, col_q:col_q + sbw].reshape(n_bs, t_s, sbw)
            k3 = proj[:, col_k:col_k + sbw].reshape(n_bs, t_s, sbw)
            v3 = proj[:, col_v:col_v + sbw].reshape(n_bs, t_s, sbw)
            k_new = k3.reshape(n_bs, t_s, n_heads, dh)
            v_new = v3.reshape(n_bs, t_s, n_heads, dh)
            y_a = _sb_attention_sample(l, q3, k3, v3, cache_k, cache_v, page_table, sb_logit_bias[l])
            y_a = y_a.reshape(rows_s, sbw).astype(BF16)
            y_b, vn = _chunk_mix(proj, blk(col_cu), blk(col_cv), cm_norm_g[l], cm_w_small[l], cm_b_small[l],
                                 rows_s, rows_s, t_s, True)
            vn = vn.reshape(n_bs, t_s, cmw)
            pad_t = lambda a: jnp.pad(a.reshape(n_bs, t_s, lw), ((0, 0), (0, SUBLANE - t_s), (0, 0))).reshape(
                n_bs * SUBLANE, lw)
            xg = jnp.concatenate([pad_t(proj[:, col_lx:col_lx + lw]), pad_t(proj[:, col_lg:col_lg + lw])], axis=1)
            y_c, tail, hs = _rglru(xg, 0, 1, n_bs, SUBLANE, SUBLANE, hist_s[l], state_lru[l],
                                   lru_conv_w[l], lru_conv_b[l], lru_w_a[l], lru_b_a[l], lru_w_x[l], lru_b_x[l],
                                   lru_lambda[l])
            y_c = y_c.reshape(n_bs, SUBLANE, lw)[:, :t_s].reshape(rows_s, lw).astype(BF16)
            t_last = t_s
        end = (t_last - 1) % SUBLANE + 1
        conv_new = tail[:, end - (CONV_W - 1):end]
        h_new = hs[:, end - 1]
        merged = _branch_merge(grp, l, y_a, y_b, y_c, proj, col_gates, w_branch)
        x = _out_proj(grp, l, merged, w_out, x)
        h, route = _moe_prep(grp, l, x, norm_ffn, rw[l], rb[l])
        x = _moe(grp, l, h, route, moe_w_gate, moe_w_up, moe_w_down, x)
        return x, (k_new, v_new, vn, conv_new, h_new)

    def trunk(grp, x, prompt):
        outs = []
        for l in range(depth):
            x, o = layer(l, grp, x, prompt)
            outs.append(o)
        y = _final_norm(x, norm_final_g, grp.tm)
        return y, [None if prompt and i == 2 else jnp.stack([o[i] for o in outs]) for i in range(5)]

    y_p, (k_p, v_p, _, conv_p, lru_p) = trunk(grp_p, x_prompt.reshape(rows_p, d), True)
    y_s, (k_s, v_s, cv_s, conv_s, lru_s) = trunk(grp_s, x_sample.reshape(rows_s, d), False)
    return (y_p.reshape(n_bp, t_p, d), y_s.reshape(n_bs, t_s, d), k_p, v_p, k_s, v_s, cv_s,
            conv_p, conv_s, lru_p, lru_s)
```

```python
import functools

import jax
import jax.numpy as jnp
from jax import lax
from jax.experimental import pallas as pl
from jax.experimental.pallas import tpu as pltpu

F32 = jnp.float32
BF16 = jnp.bfloat16

EPS = 1e-6
LRU_C = 8.0
N_MODS = 6
N_GROUPS = 4
EXPERTS_PER_GROUP = 4
N_EXPERTS = N_GROUPS * EXPERTS_PER_GROUP
CONV_W = 4
N_BRANCH = 3
LANE = 128
SUBLANE = 8
VMEM_LIMIT = 56 * 1024 * 1024


def _call(kernel, name, grid, in_specs, out_specs, out_shape, scratch=(), sem=None, n_prefetch=0):
    grid_spec = pltpu.PrefetchScalarGridSpec(
        num_scalar_prefetch=n_prefetch, grid=grid, in_specs=in_specs, out_specs=out_specs,
        scratch_shapes=list(scratch))
    return pl.pallas_call(
        kernel, grid_spec=grid_spec, out_shape=out_shape, name=name,
        compiler_params=pltpu.CompilerParams(dimension_semantics=sem, vmem_limit_bytes=VMEM_LIMIT))


def _bdot(a, b):
    return jnp.dot(a.astype(BF16), b.astype(BF16), preferred_element_type=F32)


def _split_hi_lo(x):
    hi = x.astype(BF16)
    lo = (x - hi.astype(F32)).astype(BF16)
    return hi, lo


def _hdot(a, b):
    a_hi, a_lo = _split_hi_lo(a.astype(F32))
    b_hi, b_lo = _split_hi_lo(b.astype(F32))
    dot = functools.partial(jnp.dot, preferred_element_type=F32)
    m = a.shape[0]
    if m % (2 * SUBLANE):
        return dot(a_hi, b_hi) + dot(a_lo, b_hi) + dot(a_hi, b_lo)
    both = dot(jnp.concatenate([a_hi, a_lo], axis=0), b_hi)
    return both[:m] + both[m:] + dot(a_hi, b_lo)


def _mm(a, b, precise):
    return _hdot(a, b) if precise else _bdot(a, b)


def _softplus_neg_abs(z):
    u = jnp.exp(-jnp.abs(z))
    return jnp.where(u < 1e-3, u * (1.0 - u * (0.5 - u * (1.0 / 3.0))), jnp.log(1.0 + u))


def _log_sigmoid_pair(z):
    t = _softplus_neg_abs(z)
    return jnp.minimum(z, 0.0) - t, -jnp.maximum(z, 0.0) - t


def _neg_expm1(y):
    small = -y * (1.0 + y * (0.5 + y * (1.0 / 6.0 + y * (1.0 / 24.0))))
    return jnp.where(y > -0.05, small, 1.0 - jnp.exp(y))


def _silu(x):
    return x * jax.nn.sigmoid(x)


def _gelu(x):
    return jax.nn.gelu(x, approximate=True)


def _ada_kernel(c_ref, w_ref, b_ref, o_ref):
    o_ref[...] = _hdot(_silu(c_ref[...]), w_ref[...]) + b_ref[...]


def _ada_mods(c_all, ada_w, ada_b):
    n_l, d, n = ada_w.shape
    r = c_all.shape[0]
    tn = 1024
    return _call(
        _ada_kernel, "ada_mods", (n_l, n // tn),
        [pl.BlockSpec((r, d), lambda l, j: (0, 0)),
         pl.BlockSpec((None, d, tn), lambda l, j: (l, 0, j)),
         pl.BlockSpec((None, 1, tn), lambda l, j: (l, 0, j))],
        pl.BlockSpec((None, r, tn), lambda l, j: (l, 0, j)),
        jax.ShapeDtypeStruct((n_l, r, n), F32), sem=("parallel", "parallel"),
    )(c_all, ada_w, ada_b.reshape(n_l, 1, n))


class _Group:
    def __init__(self, mods, rows, tm, rows_per_batch, per_token, precise):
        self.mods, self.rows, self.tm = mods, rows, tm
        self.rows_per_batch, self.per_token = rows_per_batch, per_token
        self.precise = precise
        self.act = F32 if precise else BF16

    def mod_spec(self, l, k, d, tn=None, tm=None):
        tn_ = d if tn is None else tn
        tm_ = self.tm if tm is None else tm
        nj = d // tn_

        def col(j):
            return k * nj + (0 if tn is None else j)

        if self.per_token:
            return pl.BlockSpec((None, tm_, tn_), lambda i, j: (l, i, col(j)))
        tpb = self.rows_per_batch // tm_
        return pl.BlockSpec((None, None, 1, tn_), lambda i, j: (l, i // tpb, 0, col(j)))


def _modulated_norm(x, g, sc, sh):
    y = x * lax.rsqrt(jnp.mean(x * x, axis=-1, keepdims=True) + EPS) * g
    return y * (1.0 + sc) + sh


def _in_proj_kernel(x_ref, g_ref, sh_ref, sc_ref, w_ref, o_ref, h_ref, *, precise):
    @pl.when(pl.program_id(1) == 0)
    def _():
        h_ref[...] = _modulated_norm(x_ref[...], g_ref[...], sc_ref[...], sh_ref[...]).astype(h_ref.dtype)

    o_ref[...] = _mm(h_ref[...], w_ref[...], precise)


def _in_proj(grp, l, x, g, w_in):
    rows, d = x.shape
    n = w_in.shape[2]
    tm, tn = grp.tm, 512
    return _call(
        functools.partial(_in_proj_kernel, precise=grp.precise), "in_proj", (rows // tm, n // tn),
        [pl.BlockSpec((tm, d), lambda i, j: (i, 0)),
         pl.BlockSpec((None, 1, d), lambda i, j: (l, 0, 0)),
         grp.mod_spec(l, 0, d), grp.mod_spec(l, 1, d),
         pl.BlockSpec((None, d, tn), lambda i, j: (l, 0, j))],
        pl.BlockSpec((tm, tn), lambda i, j: (i, j)),
        jax.ShapeDtypeStruct((rows, n), F32),
        scratch=[pltpu.VMEM((tm, d), grp.act)], sem=("parallel", "arbitrary"),
    )(x, g, grp.mods, grp.mods, w_in)


def _cumsum_matrix(n):
    j = jnp.arange(n)[:, None]
    s = jnp.arange(n)[None, :]
    return jnp.concatenate([(j > s).astype(BF16), jnp.ones((n, n), BF16)], axis=1)


def _log_sigmoid_pair_fast(z):
    t = jnp.log(1.0 + jnp.exp(-jnp.abs(z)))
    return jnp.minimum(z, 0.0) - t, -jnp.maximum(z, 0.0) - t


def _sb_prompt_kernel(bias_ref, q_ref, k_ref, v_ref, u_ref, o_ref, c_ref, acc_ref, *, tq, ck, scale):
    h = pl.program_id(1)
    qi = pl.program_id(2)
    bias = bias_ref[h]
    q = q_ref[...].astype(BF16)
    acc_ref[...] = jnp.zeros_like(acc_ref)
    c_ref[...] = jnp.zeros_like(c_ref)
    n_chunks = tq // ck
    contract_last = (((1,), (1,)), ((), ()))

    for ci in reversed(range(n_chunks)):
        r0 = ci * ck
        start = pl.multiple_of(qi * tq + r0, ck)
        kc = k_ref[pl.ds(start, ck), :].astype(BF16)
        vc = v_ref[pl.ds(start, ck), :].astype(BF16)
        z = lax.dot_general(q[r0:, :], kc, contract_last, preferred_element_type=F32) * scale + bias
        log_beta, log_keep = _log_sigmoid_pair_fast(z)
        t_pos = lax.broadcasted_iota(jnp.int32, z.shape, 0)
        s_pos = lax.broadcasted_iota(jnp.int32, z.shape, 1)
        mask = s_pos < t_pos
        log_keep = jnp.where(mask, log_keep, 0.0)
        cs = jnp.dot(log_keep.astype(BF16), u_ref[...], preferred_element_type=F32)
        c = c_ref[r0:, :]
        p = jnp.where(mask, jnp.exp(log_beta + cs[:, :ck] + c), 0.0)
        acc_ref[r0:, :] += jnp.dot(p.astype(BF16), vc, preferred_element_type=F32)
        c_ref[r0:, :] = c + cs[:, ck:]

    def body(i, carry):
        start = pl.multiple_of((qi - 1 - i) * tq, tq)
        kt = k_ref[pl.ds(start, tq), :].astype(BF16)
        vt = v_ref[pl.ds(start, tq), :].astype(BF16)
        z = lax.dot_general(q, kt, contract_last, preferred_element_type=F32) * scale + bias
        log_beta, log_keep = _log_sigmoid_pair_fast(z)
        lk = log_keep.astype(BF16)
        c = c_ref[...]
        ps = [None] * n_chunks
        for ci in reversed(range(n_chunks)):
            cols = slice(ci * ck, (ci + 1) * ck)
            cs = jnp.dot(lk[:, cols], u_ref[...], preferred_element_type=F32)
            ps[ci] = jnp.exp(log_beta[:, cols] + cs[:, :ck] + c).astype(BF16)
            c = c + cs[:, ck:]
        acc_ref[...] += jnp.dot(jnp.concatenate(ps, axis=1), vt, preferred_element_type=F32)
        c_ref[...] = c
        return carry

    lax.fori_loop(0, qi, body, 0)
    o_ref[...] = acc_ref[...].astype(o_ref.dtype)


def _sb_attention_prompt(proj, bias, n_batch, t, n_heads, dh):
    tq = min(512, t)
    ck = LANE
    nq = t // tq
    kernel = functools.partial(_sb_prompt_kernel, tq=tq, ck=ck, scale=dh ** -0.5)
    return _call(
        kernel, "sb_prompt", (n_batch, n_heads, nq),
        [pl.BlockSpec(memory_space=pltpu.SMEM),
         pl.BlockSpec((tq, dh), lambda b, h, i: (b * nq + i, h)),
         pl.BlockSpec((t, dh), lambda b, h, i: (b, n_heads + h)),
         pl.BlockSpec((t, dh), lambda b, h, i: (b, 2 * n_heads + h)),
         pl.BlockSpec((ck, 2 * ck), lambda b, h, i: (0, 0))],
        pl.BlockSpec((tq, dh), lambda b, h, i: (b * nq + i, h)),
        jax.ShapeDtypeStruct((n_batch * t, n_heads * dh), BF16),
        scratch=[pltpu.VMEM((tq, ck), F32), pltpu.VMEM((tq, dh), F32)],
        sem=("parallel", "parallel", "arbitrary"),
    )(bias, proj, proj, proj, _cumsum_matrix(ck))


def _sb_sample_kernel(pt_ref, qbd_ref, bias_ref, kn_ref, vn_ref, ut_ref, *rest, n_pp, page, n_heads, s_new, nhq,
                      scale):
    k_refs = rest[:n_pp]
    v_refs = rest[n_pp:2 * n_pp]
    o_ref, c_ref, acc_ref = rest[2 * n_pp:]
    j = pl.program_id(1)
    bias = bias_ref[...]

    def rows_by_key(ref):
        return jnp.concatenate(
            [ref[pl.ds(h, page, stride=n_heads), :].astype(BF16) for h in range(n_heads)], axis=1)

    def visit(k_list, v_list, masked):
        n = len(k_list)
        k = jnp.concatenate([rows_by_key(r) for r in k_list], axis=0)
        v = jnp.concatenate([rows_by_key(r) for r in v_list], axis=0)
        z = jnp.dot(k, qbd_ref[...], preferred_element_type=F32) * scale + bias
        log_beta, log_keep = _log_sigmoid_pair(z)
        if masked:
            s_pos = lax.broadcasted_iota(jnp.int32, z.shape, 0)
            q_pos = lax.broadcasted_iota(jnp.int32, z.shape, 1) % s_new
            mask = s_pos < q_pos
            log_keep = jnp.where(mask, log_keep, 0.0)
        hi, lo = _split_hi_lo(log_keep)
        c = c_ref[...]
        ps = [None] * n
        for i in reversed(range(n)):
            rows = slice(i * page, (i + 1) * page)
            cs = jnp.dot(ut_ref[...], jnp.concatenate([hi[rows], lo[rows]], axis=0), preferred_element_type=F32)
            ps[i] = jnp.exp(log_beta[rows] + cs + c)
            c = c + cs[0:1, :] + log_keep[i * page:i * page + 1, :]
        c_ref[...] = c
        p = ps[0] if n == 1 else jnp.concatenate(ps, axis=0)
        if masked:
            p = jnp.where(mask, p, 0.0)
        pt = jnp.transpose(p)[:nhq, :].astype(BF16)
        acc_ref[...] += jnp.dot(pt, v, preferred_element_type=F32)

    @pl.when(j == 0)
    def _():
        c_ref[...] = jnp.zeros_like(c_ref)
        acc_ref[...] = jnp.zeros_like(acc_ref)
        visit([kn_ref], [vn_ref], True)

    visit(list(k_refs)[::-1], list(v_refs)[::-1], False)

    @pl.when(j == pl.num_programs(1) - 1)
    def _():
        o_ref[...] = acc_ref[...]


def _sb_attention_sample(l, q, k_new, v_new, cache_k, cache_v, page_table, bias):
    n_b, s_new, width = q.shape
    n_l, n_phys, page, n_heads, dh = cache_k.shape
    n_pages = page_table.shape[1]
    nhq = n_heads * s_new
    n_pp = 8 if n_pages % 8 == 0 else 1
    ck = cache_k.reshape(n_l, n_phys, page * n_heads, dh)
    cv = cache_v.reshape(n_l, n_phys, page * n_heads, dh)
    q4 = q.reshape(n_b, s_new, n_heads, dh)
    qbd = jnp.einsum('bqhd,hg->bhdgq', q4, jnp.eye(n_heads, dtype=q.dtype)).reshape(n_b, width, nhq)
    qbd = jnp.pad(qbd, ((0, 0), (0, 0), (0, LANE - nhq))).astype(BF16)
    bias_l = jnp.pad(jnp.repeat(bias, s_new), (0, LANE - nhq)).reshape(1, LANE)
    kn = jnp.pad(k_new, ((0, 0), (0, page - s_new), (0, 0))).reshape(n_b, page * n_heads, dh)
    vn = jnp.pad(v_new, ((0, 0), (0, page - s_new), (0, 0))).reshape(n_b, page * n_heads, dh)
    jj = jnp.arange(page)[None, :]
    ss = jnp.arange(page)[:, None]
    ut = (jj > ss).astype(BF16)
    ut2 = jnp.concatenate([ut, ut], axis=1)

    def page_spec(i):
        return pl.BlockSpec((None, None, page * n_heads, dh),
                            lambda b, j, pt: (l, pt[b, n_pages - 1 - (j * n_pp + i)], 0, 0))

    kernel = functools.partial(_sb_sample_kernel, n_pp=n_pp, page=page, n_heads=n_heads, s_new=s_new, nhq=nhq,
                               scale=dh ** -0.5)
    out = _call(
        kernel, "sb_sample", (n_b, n_pages // n_pp),
        [pl.BlockSpec((None, width, LANE), lambda b, j, pt: (b, 0, 0)),
         pl.BlockSpec((1, LANE), lambda b, j, pt: (0, 0)),
         pl.BlockSpec((None, page * n_heads, dh), lambda b, j, pt: (b, 0, 0)),
         pl.BlockSpec((None, page * n_heads, dh), lambda b, j, pt: (b, 0, 0)),
         pl.BlockSpec((page, 2 * page), lambda b, j, pt: (0, 0))]
        + [page_spec(i) for i in range(n_pp)] + [page_spec(i) for i in range(n_pp)],
        pl.BlockSpec((None, nhq, width), lambda b, j, pt: (b, 0, 0)),
        jax.ShapeDtypeStruct((n_b, nhq, width), F32),
        scratch=[pltpu.VMEM((1, LANE), F32), pltpu.VMEM((nhq, width), F32)],
        sem=("parallel", "arbitrary"), n_prefetch=1,
    )(page_table, qbd, bias_l, kn, vn, ut2, *([ck] * n_pp), *([cv] * n_pp))
    o5 = out.reshape(n_b, n_heads, s_new, n_heads, dh)
    hh = jnp.arange(n_heads)
    diag = o5[:, hh, :, hh, :]
    return jnp.transpose(diag, (1, 2, 0, 3)).reshape(n_b, s_new, width)


def _chunk_mix_kernel(u_ref, v_ref, g_ref, w_ref, b_ref, y_ref, *vn_out, lc, period, groups, precise):
    u = _gelu(u_ref[...])
    v = _gelu(v_ref[...])
    mu = jnp.mean(v, axis=-1, keepdims=True)
    var = jnp.mean(jnp.square(v - mu), axis=-1, keepdims=True)
    vn = (v - mu) * lax.rsqrt(var + EPS) * g_ref[...]
    if vn_out:
        vn_out[0][...] = vn
    vb = vn if precise else vn.astype(BF16)
    tm, width = u.shape
    gd = width // groups
    n_ch = tm // lc
    r = lax.broadcasted_iota(jnp.int32, (lc, lc), 0)
    c = lax.broadcasted_iota(jnp.int32, (lc, lc), 1)
    mask = c <= r if period == lc else (r // period == c // period) & (c <= r)
    for g in range(groups):
        w = jnp.where(mask, w_ref[g], 0.0)
        rhs = jnp.concatenate([vb[ch * lc:(ch + 1) * lc, g * gd:(g + 1) * gd] for ch in range(n_ch)], axis=1)
        mixed = _mm(w, rhs, precise) + b_ref[g]
        for ch in range(n_ch):
            rows = slice(ch * lc, (ch + 1) * lc)
            cols = slice(g * gd, (g + 1) * gd)
            y_ref[rows, cols] = (u[rows, cols] * mixed[:, ch * gd:(ch + 1) * gd]).astype(y_ref.dtype)


def _chunk_mix(proj, col_u, col_v, g, w, b, tm, lc, period, emit_vn, precise):
    rows = proj.shape[0]
    groups = w.shape[0]
    width = g.shape[-1]
    kernel = functools.partial(_chunk_mix_kernel, lc=lc, period=period, groups=groups, precise=precise)
    out_specs = [pl.BlockSpec((tm, width), lambda i: (i, 0))]
    out_shape = [jax.ShapeDtypeStruct((rows, width), F32 if precise else BF16)]
    if emit_vn:
        out_specs.append(pl.BlockSpec((tm, width), lambda i: (i, 0)))
        out_shape.append(jax.ShapeDtypeStruct((rows, width), F32))
    return _call(
        kernel, "chunk_mix", (rows // tm,),
        [pl.BlockSpec((tm, width), lambda i: (i, col_u)),
         pl.BlockSpec((tm, width), lambda i: (i, col_v)),
         pl.BlockSpec((1, width), lambda i: (0, 0)),
         pl.BlockSpec((groups, lc, lc), lambda i: (0, 0, 0)),
         pl.BlockSpec((groups, lc, 1), lambda i: (0, 0, 0))],
        out_specs, out_shape, sem=("parallel",),
    )(proj, proj, g.reshape(1, width), w, b)


def _shift_rows(x, d, fill, row):
    tt = x.shape[0]
    if d % SUBLANE == 0:
        return jnp.concatenate([jnp.full((d,) + x.shape[1:], fill, x.dtype), x[:tt - d]], axis=0)
    return jnp.where(row < d, fill, pltpu.roll(x, d, 0))


def _rglru_kernel(x_ref, gate_ref, hist_ref, h0_ref, cw_ref, cb_ref, wa_ref, ba_ref, wx_ref, bx_ref, lam_ref,
                  y_ref, tail_ref, hs_ref, hist_s, h_s, *, precise):
    t = pl.program_id(1)

    @pl.when(t == 0)
    def _():
        hist_s[...] = hist_ref[...]
        h_s[...] = jnp.broadcast_to(h0_ref[...], h_s.shape)

    x = x_ref[...]
    tt, width = x.shape
    row = lax.broadcasted_iota(jnp.int32, (tt, width), 0)
    row8 = lax.broadcasted_iota(jnp.int32, (SUBLANE, width), 0)
    hist = hist_s[...]
    cw = cw_ref[...]
    xc = cb_ref[...] + x * cw[CONV_W - 1:CONV_W, :]
    for s in range(1, CONV_W):
        xr = pltpu.roll(x, s, 0)
        top = jnp.where(row8 < s, pltpu.roll(hist, s, 0), xr[:SUBLANE])
        xs = top if tt == SUBLANE else jnp.concatenate([top, xr[SUBLANE:]], axis=0)
        xc = xc + xs * cw[CONV_W - 1 - s:CONV_W - s, :]

    n_blk = wa_ref.shape[0]
    bd = width // n_blk
    ra, ri = [], []
    for n in range(n_blk):
        xb = xc[:, n * bd:(n + 1) * bd]
        ra.append(_mm(xb, wa_ref[n], precise))
        ri.append(_mm(xb, wx_ref[n], precise))
    r_gate = jax.nn.sigmoid(jnp.concatenate(ra, axis=1) + ba_ref[...])
    i_gate = jax.nn.sigmoid(jnp.concatenate(ri, axis=1) + bx_ref[...])
    lam = lam_ref[...]
    log_sig_lam = jnp.minimum(lam, 0.0) - _softplus_neg_abs(lam)
    log_a = LRU_C * r_gate * log_sig_lam
    a = jnp.exp(log_a)
    b = jnp.sqrt(_neg_expm1(2.0 * log_a)) * (i_gate * xc)

    d = 1
    while d < tt:
        b = a * _shift_rows(b, d, 0.0, row) + b
        a = a * _shift_rows(a, d, 1.0, row)
        d *= 2
    hs = a * h_s[SUBLANE - 1:SUBLANE, :] + b

    y_ref[...] = (hs * _gelu(gate_ref[...])).astype(y_ref.dtype)
    hist_s[...] = x[tt - SUBLANE:, :]
    h_s[...] = hs[tt - SUBLANE:, :]
    tail_ref[...] = x[tt - SUBLANE:, :]
    hs_ref[...] = hs[tt - SUBLANE:, :]


def _rglru(xsrc, col_x, col_g, n_batch, t, tt, hist8, h0, cw, cb, wa, ba, wx, bx, lam, precise):
    width = lam.shape[-1]
    nt = t // tt
    row2 = lambda a: a.reshape(1, width)
    full = lambda shape: pl.BlockSpec(shape, lambda b, i: (0,) * len(shape))
    return _call(
        functools.partial(_rglru_kernel, precise=precise), "rglru", (n_batch, nt),
        [pl.BlockSpec((tt, width), lambda b, i: (b * nt + i, col_x)),
         pl.BlockSpec((tt, width), lambda b, i: (b * nt + i, col_g)),
         pl.BlockSpec((None, SUBLANE, width), lambda b, i: (b, 0, 0)),
         pl.BlockSpec((None, 1, width), lambda b, i: (b, 0, 0)),
         full((CONV_W, width)), full((1, width)),
         full(wa.shape), full((1, width)), full(wx.shape), full((1, width)), full((1, width))],
        [pl.BlockSpec((tt, width), lambda b, i: (b * nt + i, 0)),
         pl.BlockSpec((None, SUBLANE, width), lambda b, i: (b, 0, 0)),
         pl.BlockSpec((None, SUBLANE, width), lambda b, i: (b, 0, 0))],
        [jax.ShapeDtypeStruct((n_batch * t, width), F32 if precise else BF16),
         jax.ShapeDtypeStruct((n_batch, SUBLANE, width), F32),
         jax.ShapeDtypeStruct((n_batch, SUBLANE, width), F32)],
        scratch=[pltpu.VMEM((SUBLANE, width), F32), pltpu.VMEM((SUBLANE, width), F32)],
        sem=("parallel", "arbitrary"),
    )(xsrc, xsrc, hist8, h0.reshape(n_batch, 1, width), cw, row2(cb), wa, row2(ba), wx, row2(bx), row2(lam))


def _branch_kernel(ya_ref, yb_ref, yc_ref, ga_ref, gb_ref, gc_ref, w_ref, o_ref, *, precise):
    acc = None
    for n, (y_ref, g_ref) in enumerate(((ya_ref, ga_ref), (yb_ref, gb_ref), (yc_ref, gc_ref))):
        term = jax.nn.sigmoid(g_ref[...]) * _mm(y_ref[...], w_ref[n], precise)
        acc = term if acc is None else acc + term
    o_ref[...] = acc.astype(o_ref.dtype)


def _branch_merge(grp, l, ya, yb, yc, proj, gate_col0, w_branch):
    rows, bw = ya.shape
    d = w_branch.shape[-1]
    tm, tn = grp.tm, 512
    y_spec = pl.BlockSpec((tm, bw), lambda i, j: (i, 0))

    def gate_spec(n):
        return pl.BlockSpec((tm, tn), lambda i, j: (i, (gate_col0 + n * d) // tn + j))

    return _call(
        functools.partial(_branch_kernel, precise=grp.precise), "branch_merge", (rows // tm, d // tn),
        [y_spec, y_spec, y_spec, gate_spec(0), gate_spec(1), gate_spec(2),
         pl.BlockSpec((None, N_BRANCH, bw, tn), lambda i, j: (l, 0, 0, j))],
        pl.BlockSpec((tm, tn), lambda i, j: (i, j)),
        jax.ShapeDtypeStruct((rows, d), grp.act), sem=("parallel", "arbitrary"),
    )(ya, yb, yc, proj, proj, proj, w_branch)


def _out_proj_kernel(m_ref, w_ref, x_ref, g_ref, o_ref, *, precise):
    o_ref[...] = x_ref[...] + g_ref[...] * _mm(m_ref[...], w_ref[...], precise)


def _out_proj(grp, l, merged, w_out, x):
    rows, d = x.shape
    tm, tn = grp.tm, 512
    return _call(
        functools.partial(_out_proj_kernel, precise=grp.precise), "out_proj", (rows // tm, d // tn),
        [pl.BlockSpec((tm, d), lambda i, j: (i, 0)),
         pl.BlockSpec((None, d, tn), lambda i, j: (l, 0, j)),
         pl.BlockSpec((tm, tn), lambda i, j: (i, j)),
         grp.mod_spec(l, 2, d, tn)],
        pl.BlockSpec((tm, tn), lambda i, j: (i, j)),
        jax.ShapeDtypeStruct((rows, d), F32), sem=("parallel", "arbitrary"),
    )(merged, w_out, x, grp.mods)


def _first_index_of_max(vals):
    m = functools.reduce(jnp.maximum, vals)
    idx = jnp.full(m.shape, len(vals) - 1, jnp.int32)
    for k in reversed(range(len(vals) - 1)):
        idx = jnp.where(vals[k] == m, k, idx)
    return m, idx


def _moe_prep_kernel(x_ref, g_ref, sh_ref, sc_ref, rw_ref, rb_ref, h_ref, r_ref):
    h = _modulated_norm(x_ref[...], g_ref[...], sc_ref[...], sh_ref[...])
    h_ref[...] = h.astype(h_ref.dtype)
    h_hi, h_lo = _split_hi_lo(h)
    w_hi, w_lo = _split_hi_lo(rw_ref[...])
    dot = functools.partial(jnp.dot, preferred_element_type=F32)
    logits = dot(h_hi, w_hi) + dot(h_lo, w_hi) + dot(h_hi, w_lo) + rb_ref[...]
    col = lambda k: logits[:, k:k + 1]
    gl = [col(k) for k in range(N_GROUPS)]
    g_max, g_idx = _first_index_of_max(gl)
    g_w = 1.0 / functools.reduce(jnp.add, [jnp.exp(v - g_max) for v in gl])
    e_in = []
    for j in range(EXPERTS_PER_GROUP):
        v = col(N_GROUPS + (N_GROUPS - 1) * EXPERTS_PER_GROUP + j)
        for g in reversed(range(N_GROUPS - 1)):
            v = jnp.where(g_idx == g, col(N_GROUPS + g * EXPERTS_PER_GROUP + j), v)
        e_in.append(v)
    m1, i1 = _first_index_of_max(e_in)
    rest = [jnp.where(i1 == j, -jnp.inf, e_in[j]) for j in range(EXPERTS_PER_GROUP)]
    m2, i2 = _first_index_of_max(rest)
    ratio = jnp.exp(m2 - m1)
    p1 = g_w / (1.0 + ratio)
    p2 = p1 * ratio
    lane = lax.broadcasted_iota(jnp.int32, r_ref.shape, 1)
    e_grp = lane // EXPERTS_PER_GROUP
    e_loc = lane % EXPERTS_PER_GROUP
    w = jnp.where(e_loc == i1, p1, 0.0) + jnp.where(e_loc == i2, p2, 0.0)
    r_ref[...] = jnp.where(e_grp == g_idx, w, 0.0)


def _moe_prep(grp, l, x, g, rw, rb):
    rows, d = x.shape
    tm = grp.tm
    return _call(
        _moe_prep_kernel, "moe_prep", (rows // tm, 1),
        [pl.BlockSpec((tm, d), lambda i, j: (i, 0)),
         pl.BlockSpec((None, 1, d), lambda i, j: (l, 0, 0)),
         grp.mod_spec(l, 3, d), grp.mod_spec(l, 4, d),
         pl.BlockSpec((d, LANE), lambda i, j: (0, 0)),
         pl.BlockSpec((1, LANE), lambda i, j: (0, 0))],
        [pl.BlockSpec((tm, d), lambda i, j: (i, 0)), pl.BlockSpec((tm, LANE), lambda i, j: (i, 0))],
        [jax.ShapeDtypeStruct((rows, d), grp.act), jax.ShapeDtypeStruct((rows, LANE), F32)],
        sem=("parallel", "arbitrary"),
    )(x, g, grp.mods, grp.mods, rw, rb)


def _moe_kernel(h_ref, r_ref, wg_ref, wu_ref, wd_ref, x_ref, g_ref, o_ref, *, precise):
    e = pl.program_id(1)

    @pl.when(e == 0)
    def _():
        o_ref[...] = jnp.zeros_like(o_ref)

    h = h_ref[...]
    a = _mm(h, wg_ref[...], precise)
    b = _mm(h, wu_ref[...], precise)
    lane = lax.broadcasted_iota(jnp.int32, r_ref.shape, 1)
    c = jnp.sum(jnp.where(lane == e, r_ref[...], 0.0), axis=1, keepdims=True)
    o_ref[...] += _mm(_silu(a) * b * c, wd_ref[...], precise)

    @pl.when(e == pl.num_programs(1) - 1)
    def _():
        o_ref[...] = x_ref[...] + g_ref[...] * o_ref[...]


def _moe(grp, l, h, route, w_gate, w_up, w_down, x):
    rows, d = x.shape
    n_e, _, f = w_gate.shape[1:]
    tm = min(grp.tm, 512)
    return _call(
        functools.partial(_moe_kernel, precise=grp.precise), "moe", (rows // tm, n_e),
        [pl.BlockSpec((tm, d), lambda i, e: (i, 0)),
         pl.BlockSpec((tm, LANE), lambda i, e: (i, 0)),
         pl.BlockSpec((None, None, d, f), lambda i, e: (l, e, 0, 0)),
         pl.BlockSpec((None, None, d, f), lambda i, e: (l, e, 0, 0)),
         pl.BlockSpec((None, None, f, d), lambda i, e: (l, e, 0, 0)),
         pl.BlockSpec((tm, d), lambda i, e: (i, 0)),
         grp.mod_spec(l, 5, d, tm=tm)],
        pl.BlockSpec((tm, d), lambda i, e: (i, 0)),
        jax.ShapeDtypeStruct((rows, d), F32), sem=("parallel", "arbitrary"),
    )(h, route, w_gate, w_up, w_down, x, grp.mods)


def _final_norm_kernel(x_ref, g_ref, o_ref):
    x = x_ref[...]
    o_ref[...] = x * lax.rsqrt(jnp.mean(x * x, axis=-1, keepdims=True) + EPS) * g_ref[...]


def _final_norm(x, g, tm):
    rows, d = x.shape
    return _call(
        _final_norm_kernel, "final_norm", (rows // tm,),
        [pl.BlockSpec((tm, d), lambda i: (i, 0)), pl.BlockSpec((1, d), lambda i: (0, 0))],
        pl.BlockSpec((tm, d), lambda i: (i, 0)),
        jax.ShapeDtypeStruct((rows, d), F32), sem=("parallel",),
    )(x, g.reshape(1, d))


def kernel(x_prompt, x_sample, cache_k, cache_v, state_conv, state_lru, page_table, c_prompt, c_sample, ada_w, ada_b, norm_mix_g, norm_ffn_g, norm_final_g, w_in, sb_logit_bias, cm_norm_g, cm_w_s, cm_b_s, lru_conv_w, lru_conv_b, lru_w_a, lru_b_a, lru_w_x, lru_b_x, lru_lambda, w_branch, w_out, router_group_w, router_group_b, router_expert_w, router_expert_b, moe_w_gate, moe_w_up, moe_w_down):
    n_bp, t_p, d = x_prompt.shape
    n_bs, t_s, _ = x_sample.shape
    depth = w_in.shape[0]
    n_heads, dh = cache_k.shape[3], cache_k.shape[4]
    sbw = n_heads * dh
    cmw = cm_norm_g.shape[-1]
    lw = lru_lambda.shape[-1]
    chunk = cm_w_s.shape[-1]
    assert t_s >= CONV_W - 1 and t_s <= SUBLANE and t_p % chunk == 0
    col_q, col_k, col_v = 0, sbw, 2 * sbw
    col_cu, col_cv = 3 * sbw, 3 * sbw + cmw
    col_lx, col_lg = 3 * sbw + 2 * cmw, 3 * sbw + 2 * cmw + lw
    col_gates = 3 * sbw + 2 * cmw + 2 * lw
    assert cmw == sbw and lw == sbw

    n_c = n_bp + n_bs
    c_rows = -(-n_c // SUBLANE) * SUBLANE
    c_all = jnp.pad(jnp.concatenate([c_prompt, c_sample], axis=0), ((0, c_rows - n_c), (0, 0)))
    mods = _ada_mods(c_all, ada_w, ada_b)
    rows_p, rows_s = n_bp * t_p, n_bs * t_s
    tm_p = min(1024, t_p)
    grp_p = _Group(mods[:, :n_bp].reshape(depth, n_bp, 1, N_MODS * d), rows_p, tm_p, t_p, False, False)
    grp_s = _Group(jnp.repeat(mods[:, n_bp:n_c], t_s, axis=1), rows_s, rows_s, t_s, True, True)

    norm_mix = norm_mix_g.reshape(depth, 1, d)
    norm_ffn = norm_ffn_g.reshape(depth, 1, d)
    rw = jnp.pad(jnp.concatenate([router_group_w, router_expert_w], axis=-1),
                 ((0, 0), (0, 0), (0, LANE - N_GROUPS - N_EXPERTS)))
    rb = jnp.pad(jnp.concatenate([router_group_b, router_expert_b], axis=-1),
                 ((0, 0), (0, LANE - N_GROUPS - N_EXPERTS))).reshape(depth, 1, LANE)

    n_rep = rows_s // t_s
    cm_w_small = jnp.tile(cm_w_s[:, :, :t_s, :t_s], (1, 1, n_rep, n_rep))
    cm_b_small = jnp.tile(cm_b_s[:, :, :t_s], (1, 1, n_rep))[..., None]
    cm_b_full = cm_b_s[..., None]
    zero_hist = jnp.zeros((n_bp, SUBLANE, lw), F32)
    zero_h = jnp.zeros((n_bp, lw), F32)
    hist_s = jnp.pad(state_conv, ((0, 0), (0, 0), (SUBLANE - (CONV_W - 1), 0), (0, 0)))
    tt_p = min(256, t_p)
    tm_cm = min(512, t_p)

    def layer(l, grp, x, prompt):
        proj = _in_proj(grp, l, x, norm_mix, w_in)
        blk = lambda c: c // sbw
        if prompt:
            k_new = proj[:, col_k:col_k + sbw].reshape(n_bp, t_p, n_heads, dh)
            v_new = proj[:, col_v:col_v + sbw].reshape(n_bp, t_p, n_heads, dh)
            y_a = _sb_attention_prompt(proj, sb_logit_bias[l], n_bp, t_p, n_heads, dh)
            y_b, = _chunk_mix(proj, blk(col_cu), blk(col_cv), cm_norm_g[l], cm_w_s[l], cm_b_full[l],
                              tm_cm, chunk, chunk, False, False)
            vn = None
            y_c, tail, hs = _rglru(proj, blk(col_lx), blk(col_lg), n_bp, t_p, tt_p, zero_hist, zero_h,
                                   lru_conv_w[l], lru_conv_b[l], lru_w_a[l], lru_b_a[l], lru_w_x[l], lru_b_x[l],
                                   lru_lambda[l], False)
            t_last = t_p
        else:
            q3 = proj[:, col_q:col_q + sbw].reshape(n_bs, t_s, sbw)
            k3 = proj[:, col_k:col_k + sbw].reshape(n_bs, t_s, sbw)
            v3 = proj[:, col_v:col_v + sbw].reshape(n_bs, t_s, sbw)
            k_new = k3.reshape(n_bs, t_s, n_heads, dh)
            v_new = v3.reshape(n_bs, t_s, n_heads, dh)
            y_a = _sb_attention_sample(l, q3, k3, v3, cache_k, cache_v, page_table, sb_logit_bias[l])
            y_a = y_a.reshape(rows_s, sbw)
            y_b, vn = _chunk_mix(proj, blk(col_cu), blk(col_cv), cm_norm_g[l], cm_w_small[l], cm_b_small[l],
                                 rows_s, rows_s, t_s, True, True)
            vn = vn.reshape(n_bs, t_s, cmw)
            pad_t = lambda a: jnp.pad(a.reshape(n_bs, t_s, lw), ((0, 0), (0, SUBLANE - t_s), (0, 0))).reshape(
                n_bs * SUBLANE, lw)
            xg = jnp.concatenate([pad_t(proj[:, col_lx:col_lx + lw]), pad_t(proj[:, col_lg:col_lg + lw])], axis=1)
            y_c, tail, hs = _rglru(xg, 0, 1, n_bs, SUBLANE, SUBLANE, hist_s[l], state_lru[l],
                                   lru_conv_w[l], lru_conv_b[l], lru_w_a[l], lru_b_a[l], lru_w_x[l], lru_b_x[l],
                                   lru_lambda[l], True)
            y_c = y_c.reshape(n_bs, SUBLANE, lw)[:, :t_s].reshape(rows_s, lw)
            t_last = t_s
        end = (t_last - 1) % SUBLANE + 1
        conv_new = tail[:, end - (CONV_W - 1):end]
        h_new = hs[:, end - 1]
        merged = _branch_merge(grp, l, y_a, y_b, y_c, proj, col_gates, w_branch)
        x = _out_proj(grp, l, merged, w_out, x)
        h, route = _moe_prep(grp, l, x, norm_ffn, rw[l], rb[l])
        x = _moe(grp, l, h, route, moe_w_gate, moe_w_up, moe_w_down, x)
        return x, (k_new, v_new, vn, conv_new, h_new)

    def trunk(grp, x, prompt):
        outs = []
        for l in range(depth):
            x, o = layer(l, grp, x, prompt)
            outs.append(o)
        y = _final_norm(x, norm_final_g, grp.tm)
        return y, [None if prompt and i == 2 else jnp.stack([o[i] for o in outs]) for i in range(5)]

    y_p, (k_p, v_p, _, conv_p, lru_p) = trunk(grp_p, x_prompt.reshape(rows_p, d), True)
    y_s, (k_s, v_s, cv_s, conv_s, lru_s) = trunk(grp_s, x_sample.reshape(rows_s, d), False)
    return (y_p.reshape(n_bp, t_p, d), y_s.reshape(n_bs, t_s, d), k_p, v_p, k_s, v_s, cv_s,
            conv_p, conv_s, lru_p, lru_s)
```

```python
import functools

import jax
import jax.numpy as jnp
from jax import lax
from jax.experimental import pallas as pl
from jax.experimental.pallas import tpu as pltpu

F32 = jnp.float32
BF16 = jnp.bfloat16

EPS = 1e-6
LRU_C = 8.0
N_MODS = 6
N_GROUPS = 4
EXPERTS_PER_GROUP = 4
N_EXPERTS = N_GROUPS * EXPERTS_PER_GROUP
CONV_W = 4
N_BRANCH = 3
LANE = 128
SUBLANE = 8
VMEM_LIMIT = 56 * 1024 * 1024


def _call(kernel, name, grid, in_specs, out_specs, out_shape, scratch=(), sem=None, n_prefetch=0):
    grid_spec = pltpu.PrefetchScalarGridSpec(
        num_scalar_prefetch=n_prefetch, grid=grid, in_specs=in_specs, out_specs=out_specs,
        scratch_shapes=list(scratch))
    return pl.pallas_call(
        kernel, grid_spec=grid_spec, out_shape=out_shape, name=name,
        compiler_params=pltpu.CompilerParams(dimension_semantics=sem, vmem_limit_bytes=VMEM_LIMIT))


def _bdot(a, b):
    return jnp.dot(a.astype(BF16), b.astype(BF16), preferred_element_type=F32)


def _split_hi_lo(x):
    hi = x.astype(BF16)
    lo = (x - hi.astype(F32)).astype(BF16)
    return hi, lo


def _hdot(a, b):
    a_hi, a_lo = _split_hi_lo(a.astype(F32))
    b_hi, b_lo = _split_hi_lo(b.astype(F32))
    dot = functools.partial(jnp.dot, preferred_element_type=F32)
    m = a.shape[0]
    if m % (2 * SUBLANE):
        return dot(a_hi, b_hi) + dot(a_lo, b_hi) + dot(a_hi, b_lo)
    both = dot(jnp.concatenate([a_hi, a_lo], axis=0), b_hi)
    return both[:m] + both[m:] + dot(a_hi, b_lo)


def _mm(a, b, precise):
    return _hdot(a, b) if precise else _bdot(a, b)


def _softplus_neg_abs(z):
    u = jnp.exp(-jnp.abs(z))
    return jnp.where(u < 1e-3, u * (1.0 - u * (0.5 - u * (1.0 / 3.0))), jnp.log(1.0 + u))


def _log_sigmoid_pair(z):
    t = _softplus_neg_abs(z)
    return jnp.minimum(z, 0.0) - t, -jnp.maximum(z, 0.0) - t


def _neg_expm1(y):
    small = -y * (1.0 + y * (0.5 + y * (1.0 / 6.0 + y * (1.0 / 24.0))))
    return jnp.where(y > -0.05, small, 1.0 - jnp.exp(y))


def _silu(x):
    return x * jax.nn.sigmoid(x)


def _gelu(x):
    return jax.nn.gelu(x, approximate=True)


def _ada_kernel(c_ref, w_ref, b_ref, o_ref):
    o_ref[...] = _hdot(_silu(c_ref[...]), w_ref[...]) + b_ref[...]


def _ada_mods(c_all, ada_w, ada_b):
    n_l, d, n = ada_w.shape
    r = c_all.shape[0]
    tn = 1024
    return _call(
        _ada_kernel, "ada_mods", (n_l, n // tn),
        [pl.BlockSpec((r, d), lambda l, j: (0, 0)),
         pl.BlockSpec((None, d, tn), lambda l, j: (l, 0, j)),
         pl.BlockSpec((None, 1, tn), lambda l, j: (l, 0, j))],
        pl.BlockSpec((None, r, tn), lambda l, j: (l, 0, j)),
        jax.ShapeDtypeStruct((n_l, r, n), F32), sem=("parallel", "parallel"),
    )(c_all, ada_w, ada_b.reshape(n_l, 1, n))


class _Group:
    def __init__(self, mods, rows, tm, rows_per_batch, per_token, precise):
        self.mods, self.rows, self.tm = mods, rows, tm
        self.rows_per_batch, self.per_token = rows_per_batch, per_token
        self.precise = precise
        self.act = F32 if precise else BF16

    def mod_spec(self, l, k, d, tn=None, tm=None):
        tn_ = d if tn is None else tn
        tm_ = self.tm if tm is None else tm
        nj = d // tn_

        def col(j):
            return k * nj + (0 if tn is None else j)

        if self.per_token:
            return pl.BlockSpec((None, tm_, tn_), lambda i, j, *_: (l, i, col(j)))
        tpb = self.rows_per_batch // tm_
        return pl.BlockSpec((None, None, 1, tn_), lambda i, j, *_: (l, i // tpb, 0, col(j)))


def _modulated_norm(x, g, sc, sh):
    y = x * lax.rsqrt(jnp.mean(x * x, axis=-1, keepdims=True) + EPS) * g
    return y * (1.0 + sc) + sh


def _norm_mod_kernel(x_ref, g_ref, sh_ref, sc_ref, o_ref):
    o_ref[...] = _modulated_norm(x_ref[...], g_ref[...], sc_ref[...], sh_ref[...]).astype(o_ref.dtype)


def _norm_mod(grp, l, x, g):
    rows, d = x.shape
    tm = min(grp.tm, 512)
    return _call(
        _norm_mod_kernel, "norm_mod", (rows // tm, 1),
        [pl.BlockSpec((tm, d), lambda i, j: (i, 0)),
         pl.BlockSpec((None, 1, d), lambda i, j: (l, 0, 0)),
         grp.mod_spec(l, 0, d, tm=tm), grp.mod_spec(l, 1, d, tm=tm)],
        pl.BlockSpec((tm, d), lambda i, j: (i, 0)),
        jax.ShapeDtypeStruct((rows, d), grp.act), sem=("parallel", "arbitrary"),
    )(x, g, grp.mods, grp.mods)


def _in_proj_kernel(h_ref, w_ref, o_ref, *kv_ref, precise, kv_lo, kv_hi):
    y = _mm(h_ref[...], w_ref[...], precise)
    o_ref[...] = y.astype(o_ref.dtype)
    if kv_ref:
        j = pl.program_id(1)

        @pl.when((j >= kv_lo) & (j < kv_hi))
        def _():
            kv_ref[0][...] = y


def _in_proj(grp, l, h, w_in, kv_col0, kv_cols):
    rows, d = h.shape
    n = w_in.shape[2]
    tm = min(2 * grp.tm, grp.rows_per_batch) if not grp.per_token else grp.tm
    tn = 512
    kv_lo, kv_hi = kv_col0 // tn, (kv_col0 + kv_cols) // tn
    out_specs = [pl.BlockSpec((tm, tn), lambda i, j: (i, j))]
    out_shape = [jax.ShapeDtypeStruct((rows, n), grp.act)]
    if not grp.precise:
        out_specs.append(pl.BlockSpec((tm, tn), lambda i, j: (i, jnp.clip(j - kv_lo, 0, kv_hi - kv_lo - 1))))
        out_shape.append(jax.ShapeDtypeStruct((rows, kv_cols), F32))
    return _call(
        functools.partial(_in_proj_kernel, precise=grp.precise, kv_lo=kv_lo, kv_hi=kv_hi), "in_proj",
        (rows // tm, n // tn),
        [pl.BlockSpec((tm, d), lambda i, j: (i, 0)),
         pl.BlockSpec((None, d, tn), lambda i, j: (l, 0, j))],
        out_specs, out_shape, sem=("parallel", "arbitrary"),
    )(h, w_in)


def _cumsum_matrix(n):
    j = jnp.arange(n)[:, None]
    s = jnp.arange(n)[None, :]
    return jnp.concatenate([(j > s).astype(BF16), jnp.ones((n, n), BF16)], axis=1)


def _log_sigmoid_pair_fast(z):
    t = jnp.log(1.0 + jnp.exp(-jnp.abs(z)))
    return jnp.minimum(z, 0.0) - t, -jnp.maximum(z, 0.0) - t


def _sb_prompt_kernel(bias_ref, q_ref, k_ref, v_ref, u_ref, o_ref, c_ref, acc_ref, *, tq, ck, scale):
    h = pl.program_id(1)
    qi = pl.program_id(2)
    bias = bias_ref[h]
    q = q_ref[...].astype(BF16)
    acc_ref[...] = jnp.zeros_like(acc_ref)
    c_ref[...] = jnp.zeros_like(c_ref)
    n_chunks = tq // ck
    contract_last = (((1,), (1,)), ((), ()))

    for ci in reversed(range(n_chunks)):
        r0 = ci * ck
        start = pl.multiple_of(qi * tq + r0, ck)
        kc = k_ref[pl.ds(start, ck), :].astype(BF16)
        vc = v_ref[pl.ds(start, ck), :].astype(BF16)
        z = lax.dot_general(q[r0:, :], kc, contract_last, preferred_element_type=F32) * scale + bias
        log_beta, log_keep = _log_sigmoid_pair_fast(z)
        t_pos = lax.broadcasted_iota(jnp.int32, z.shape, 0)
        s_pos = lax.broadcasted_iota(jnp.int32, z.shape, 1)
        mask = s_pos < t_pos
        log_keep = jnp.where(mask, log_keep, 0.0)
        cs = jnp.dot(log_keep.astype(BF16), u_ref[...], preferred_element_type=F32)
        c = c_ref[r0:, :]
        p = jnp.where(mask, jnp.exp(log_beta + cs[:, :ck] + c), 0.0)
        acc_ref[r0:, :] += jnp.dot(p.astype(BF16), vc, preferred_element_type=F32)
        c_ref[r0:, :] = c + cs[:, ck:]

    def body(i, carry):
        start = pl.multiple_of((qi - 1 - i) * tq, tq)
        kt = k_ref[pl.ds(start, tq), :].astype(BF16)
        vt = v_ref[pl.ds(start, tq), :].astype(BF16)
        z = lax.dot_general(q, kt, contract_last, preferred_element_type=F32) * scale + bias
        log_beta, log_keep = _log_sigmoid_pair_fast(z)
        lk = log_keep.astype(BF16)
        c = c_ref[...]
        ps = [None] * n_chunks
        for ci in reversed(range(n_chunks)):
            cols = slice(ci * ck, (ci + 1) * ck)
            cs = jnp.dot(lk[:, cols], u_ref[...], preferred_element_type=F32)
            ps[ci] = jnp.exp(log_beta[:, cols] + cs[:, :ck] + c).astype(BF16)
            c = c + cs[:, ck:]
        acc_ref[...] += jnp.dot(jnp.concatenate(ps, axis=1), vt, preferred_element_type=F32)
        c_ref[...] = c
        return carry

    lax.fori_loop(0, qi, body, 0)
    o_ref[...] = acc_ref[...].astype(o_ref.dtype)


def _sb_attention_prompt(proj, bias, n_batch, t, n_heads, dh):
    tq = min(512, t)
    ck = LANE
    nq = t // tq
    kernel = functools.partial(_sb_prompt_kernel, tq=tq, ck=ck, scale=dh ** -0.5)
    return _call(
        kernel, "sb_prompt", (n_batch, n_heads, nq),
        [pl.BlockSpec(memory_space=pltpu.SMEM),
         pl.BlockSpec((tq, dh), lambda b, h, i: (b * nq + i, h)),
         pl.BlockSpec((t, dh), lambda b, h, i: (b, n_heads + h)),
         pl.BlockSpec((t, dh), lambda b, h, i: (b, 2 * n_heads + h)),
         pl.BlockSpec((ck, 2 * ck), lambda b, h, i: (0, 0))],
        pl.BlockSpec((tq, dh), lambda b, h, i: (b * nq + i, h)),
        jax.ShapeDtypeStruct((n_batch * t, n_heads * dh), BF16),
        scratch=[pltpu.VMEM((tq, ck), F32), pltpu.VMEM((tq, dh), F32)],
        sem=("parallel", "parallel", "arbitrary"),
    )(bias, proj, proj, proj, _cumsum_matrix(ck))


def _sb_sample_kernel(pt_ref, qbd_ref, bias_ref, kn_ref, vn_ref, ut_ref, *rest, n_pp, page, n_heads, s_new, nhq,
                      scale):
    k_refs = rest[:n_pp]
    v_refs = rest[n_pp:2 * n_pp]
    o_ref, c_ref, acc_ref = rest[2 * n_pp:]
    j = pl.program_id(1)
    bias = bias_ref[...]

    def rows_by_key(ref):
        return jnp.concatenate(
            [ref[pl.ds(h, page, stride=n_heads), :].astype(BF16) for h in range(n_heads)], axis=1)

    def visit(k_list, v_list, masked):
        n = len(k_list)
        k = jnp.concatenate([rows_by_key(r) for r in k_list], axis=0)
        v = jnp.concatenate([rows_by_key(r) for r in v_list], axis=0)
        z = jnp.dot(k, qbd_ref[...], preferred_element_type=F32) * scale + bias
        log_beta, log_keep = _log_sigmoid_pair(z)
        if masked:
            s_pos = lax.broadcasted_iota(jnp.int32, z.shape, 0)
            q_pos = lax.broadcasted_iota(jnp.int32, z.shape, 1) % s_new
            mask = s_pos < q_pos
            log_keep = jnp.where(mask, log_keep, 0.0)
        hi, lo = _split_hi_lo(log_keep)
        c = c_ref[...]
        ps = [None] * n
        for i in reversed(range(n)):
            rows = slice(i * page, (i + 1) * page)
            cs = jnp.dot(ut_ref[...], jnp.concatenate([hi[rows], lo[rows]], axis=0), preferred_element_type=F32)
            ps[i] = jnp.exp(log_beta[rows] + cs + c)
            c = c + cs[0:1, :] + log_keep[i * page:i * page + 1, :]
        c_ref[...] = c
        p = ps[0] if n == 1 else jnp.concatenate(ps, axis=0)
        if masked:
            p = jnp.where(mask, p, 0.0)
        pt = jnp.transpose(p)[:nhq, :].astype(BF16)
        acc_ref[...] += jnp.dot(pt, v, preferred_element_type=F32)

    @pl.when(j == 0)
    def _():
        c_ref[...] = jnp.zeros_like(c_ref)
        acc_ref[...] = jnp.zeros_like(acc_ref)
        visit([kn_ref], [vn_ref], True)

    visit(list(k_refs)[::-1], list(v_refs)[::-1], False)

    @pl.when(j == pl.num_programs(1) - 1)
    def _():
        o_ref[...] = acc_ref[...]


def _sb_attention_sample(l, q, k_new, v_new, cache_k, cache_v, page_table, bias):
    n_b, s_new, width = q.shape
    n_l, n_phys, page, n_heads, dh = cache_k.shape
    n_pages = page_table.shape[1]
    nhq = n_heads * s_new
    n_pp = 8 if n_pages % 8 == 0 else 1
    ck = cache_k.reshape(n_l, n_phys, page * n_heads, dh)
    cv = cache_v.reshape(n_l, n_phys, page * n_heads, dh)
    q4 = q.reshape(n_b, s_new, n_heads, dh)
    qbd = jnp.einsum('bqhd,hg->bhdgq', q4, jnp.eye(n_heads, dtype=q.dtype)).reshape(n_b, width, nhq)
    qbd = jnp.pad(qbd, ((0, 0), (0, 0), (0, LANE - nhq))).astype(BF16)
    bias_l = jnp.pad(jnp.repeat(bias, s_new), (0, LANE - nhq)).reshape(1, LANE)
    kn = jnp.pad(k_new, ((0, 0), (0, page - s_new), (0, 0))).reshape(n_b, page * n_heads, dh)
    vn = jnp.pad(v_new, ((0, 0), (0, page - s_new), (0, 0))).reshape(n_b, page * n_heads, dh)
    jj = jnp.arange(page)[None, :]
    ss = jnp.arange(page)[:, None]
    ut = (jj > ss).astype(BF16)
    ut2 = jnp.concatenate([ut, ut], axis=1)

    def page_spec(i):
        return pl.BlockSpec((None, None, page * n_heads, dh),
                            lambda b, j, pt: (l, pt[b, n_pages - 1 - (j * n_pp + i)], 0, 0))

    kernel = functools.partial(_sb_sample_kernel, n_pp=n_pp, page=page, n_heads=n_heads, s_new=s_new, nhq=nhq,
                               scale=dh ** -0.5)
    out = _call(
        kernel, "sb_sample", (n_b, n_pages // n_pp),
        [pl.BlockSpec((None, width, LANE), lambda b, j, pt: (b, 0, 0)),
         pl.BlockSpec((1, LANE), lambda b, j, pt: (0, 0)),
         pl.BlockSpec((None, page * n_heads, dh), lambda b, j, pt: (b, 0, 0)),
         pl.BlockSpec((None, page * n_heads, dh), lambda b, j, pt: (b, 0, 0)),
         pl.BlockSpec((page, 2 * page), lambda b, j, pt: (0, 0))]
        + [page_spec(i) for i in range(n_pp)] + [page_spec(i) for i in range(n_pp)],
        pl.BlockSpec((None, nhq, width), lambda b, j, pt: (b, 0, 0)),
        jax.ShapeDtypeStruct((n_b, nhq, width), F32),
        scratch=[pltpu.VMEM((1, LANE), F32), pltpu.VMEM((nhq, width), F32)],
        sem=("parallel", "arbitrary"), n_prefetch=1,
    )(page_table, qbd, bias_l, kn, vn, ut2, *([ck] * n_pp), *([cv] * n_pp))
    o5 = out.reshape(n_b, n_heads, s_new, n_heads, dh)
    hh = jnp.arange(n_heads)
    diag = o5[:, hh, :, hh, :]
    return jnp.transpose(diag, (1, 2, 0, 3)).reshape(n_b, s_new, width)


def _chunk_mix_kernel(u_ref, v_ref, g_ref, w_ref, b_ref, y_ref, *vn_out, lc, period, groups, precise):
    u = _gelu(u_ref[...].astype(F32))
    v = _gelu(v_ref[...].astype(F32))
    mu = jnp.mean(v, axis=-1, keepdims=True)
    var = jnp.mean(jnp.square(v - mu), axis=-1, keepdims=True)
    vn = (v - mu) * lax.rsqrt(var + EPS) * g_ref[...]
    if vn_out:
        vn_out[0][...] = vn
    vb = vn if precise else vn.astype(BF16)
    tm, width = u.shape
    gd = width // groups
    n_ch = tm // lc
    r = lax.broadcasted_iota(jnp.int32, (lc, lc), 0)
    c = lax.broadcasted_iota(jnp.int32, (lc, lc), 1)
    mask = c <= r if period == lc else (r // period == c // period) & (c <= r)
    for g in range(groups):
        w = jnp.where(mask, w_ref[g], 0.0)
        rhs = jnp.concatenate([vb[ch * lc:(ch + 1) * lc, g * gd:(g + 1) * gd] for ch in range(n_ch)], axis=1)
        mixed = _mm(w, rhs, precise) + b_ref[g]
        for ch in range(n_ch):
            rows = slice(ch * lc, (ch + 1) * lc)
            cols = slice(g * gd, (g + 1) * gd)
            y_ref[rows, cols] = (u[rows, cols] * mixed[:, ch * gd:(ch + 1) * gd]).astype(y_ref.dtype)


def _chunk_mix(proj, col_u, col_v, g, w, b, tm, lc, period, emit_vn, precise):
    rows = proj.shape[0]
    groups = w.shape[0]
    width = g.shape[-1]
    kernel = functools.partial(_chunk_mix_kernel, lc=lc, period=period, groups=groups, precise=precise)
    out_specs = [pl.BlockSpec((tm, width), lambda i: (i, 0))]
    out_shape = [jax.ShapeDtypeStruct((rows, width), F32 if precise else BF16)]
    if emit_vn:
        out_specs.append(pl.BlockSpec((tm, width), lambda i: (i, 0)))
        out_shape.append(jax.ShapeDtypeStruct((rows, width), F32))
    return _call(
        kernel, "chunk_mix", (rows // tm,),
        [pl.BlockSpec((tm, width), lambda i: (i, col_u)),
         pl.BlockSpec((tm, width), lambda i: (i, col_v)),
         pl.BlockSpec((1, width), lambda i: (0, 0)),
         pl.BlockSpec((groups, lc, lc), lambda i: (0, 0, 0)),
         pl.BlockSpec((groups, lc, 1), lambda i: (0, 0, 0))],
        out_specs, out_shape, sem=("parallel",),
    )(proj, proj, g.reshape(1, width), w, b)


def _shift_rows(x, d, fill, row):
    tt = x.shape[0]
    if d % SUBLANE == 0:
        return jnp.concatenate([jnp.full((d,) + x.shape[1:], fill, x.dtype), x[:tt - d]], axis=0)
    return jnp.where(row < d, fill, pltpu.roll(x, d, 0))


def _rglru_kernel(x_ref, gate_ref, hist_ref, h0_ref, cw_ref, cb_ref, wa_ref, ba_ref, wx_ref, bx_ref, lam_ref,
                  y_ref, tail_ref, hs_ref, hist_s, h_s, *, precise):
    t = pl.program_id(1)

    @pl.when(t == 0)
    def _():
        hist_s[...] = hist_ref[...]
        h_s[...] = jnp.broadcast_to(h0_ref[...], h_s.shape)

    x = x_ref[...].astype(F32)
    tt, width = x.shape
    row = lax.broadcasted_iota(jnp.int32, (tt, width), 0)
    row8 = lax.broadcasted_iota(jnp.int32, (SUBLANE, width), 0)
    hist = hist_s[...]
    cw = cw_ref[...]
    xc = cb_ref[...] + x * cw[CONV_W - 1:CONV_W, :]
    for s in range(1, CONV_W):
        xr = pltpu.roll(x, s, 0)
        top = jnp.where(row8 < s, pltpu.roll(hist, s, 0), xr[:SUBLANE])
        xs = top if tt == SUBLANE else jnp.concatenate([top, xr[SUBLANE:]], axis=0)
        xc = xc + xs * cw[CONV_W - 1 - s:CONV_W - s, :]

    n_blk = wa_ref.shape[0]
    bd = width // n_blk
    ra, ri = [], []
    for n in range(n_blk):
        xb = xc[:, n * bd:(n + 1) * bd]
        ra.append(_mm(xb, wa_ref[n], precise))
        ri.append(_mm(xb, wx_ref[n], precise))
    r_gate = jax.nn.sigmoid(jnp.concatenate(ra, axis=1) + ba_ref[...])
    i_gate = jax.nn.sigmoid(jnp.concatenate(ri, axis=1) + bx_ref[...])
    lam = lam_ref[...]
    log_sig_lam = jnp.minimum(lam, 0.0) - _softplus_neg_abs(lam)
    log_a = LRU_C * r_gate * log_sig_lam
    a = jnp.exp(log_a)
    b = jnp.sqrt(_neg_expm1(2.0 * log_a)) * (i_gate * xc)

    d = 1
    while d < tt:
        b = a * _shift_rows(b, d, 0.0, row) + b
        a = a * _shift_rows(a, d, 1.0, row)
        d *= 2
    hs = a * h_s[SUBLANE - 1:SUBLANE, :] + b

    y_ref[...] = (hs * _gelu(gate_ref[...].astype(F32))).astype(y_ref.dtype)
    hist_s[...] = x[tt - SUBLANE:, :]
    h_s[...] = hs[tt - SUBLANE:, :]
    tail_ref[...] = x[tt - SUBLANE:, :]
    hs_ref[...] = hs[tt - SUBLANE:, :]


def _rglru(xsrc, col_x, col_g, n_batch, t, tt, hist8, h0, cw, cb, wa, ba, wx, bx, lam, precise):
    width = lam.shape[-1]
    nt = t // tt
    row2 = lambda a: a.reshape(1, width)
    full = lambda shape: pl.BlockSpec(shape, lambda b, i: (0,) * len(shape))
    return _call(
        functools.partial(_rglru_kernel, precise=precise), "rglru", (n_batch, nt),
        [pl.BlockSpec((tt, width), lambda b, i: (b * nt + i, col_x)),
         pl.BlockSpec((tt, width), lambda b, i: (b * nt + i, col_g)),
         pl.BlockSpec((None, SUBLANE, width), lambda b, i: (b, 0, 0)),
         pl.BlockSpec((None, 1, width), lambda b, i: (b, 0, 0)),
         full((CONV_W, width)), full((1, width)),
         full(wa.shape), full((1, width)), full(wx.shape), full((1, width)), full((1, width))],
        [pl.BlockSpec((tt, width), lambda b, i: (b * nt + i, 0)),
         pl.BlockSpec((None, SUBLANE, width), lambda b, i: (b, 0, 0)),
         pl.BlockSpec((None, SUBLANE, width), lambda b, i: (b, 0, 0))],
        [jax.ShapeDtypeStruct((n_batch * t, width), F32 if precise else BF16),
         jax.ShapeDtypeStruct((n_batch, SUBLANE, width), F32),
         jax.ShapeDtypeStruct((n_batch, SUBLANE, width), F32)],
        scratch=[pltpu.VMEM((SUBLANE, width), F32), pltpu.VMEM((SUBLANE, width), F32)],
        sem=("parallel", "arbitrary"),
    )(xsrc, xsrc, hist8, h0.reshape(n_batch, 1, width), cw, row2(cb), wa, row2(ba), wx, row2(bx), row2(lam))


def _branch_kernel(ya_ref, yb_ref, yc_ref, ga_ref, gb_ref, gc_ref, w_ref, o_ref, *, precise):
    acc = None
    for n, (y_ref, g_ref) in enumerate(((ya_ref, ga_ref), (yb_ref, gb_ref), (yc_ref, gc_ref))):
        term = jax.nn.sigmoid(g_ref[...].astype(F32)) * _mm(y_ref[...], w_ref[n], precise)
        acc = term if acc is None else acc + term
    o_ref[...] = acc.astype(o_ref.dtype)


def _branch_merge(grp, l, ya, yb, yc, proj, gate_col0, w_branch):
    rows, bw = ya.shape
    d = w_branch.shape[-1]
    tm, tn = grp.tm, 512
    y_spec = pl.BlockSpec((tm, bw), lambda i, j: (i, 0))

    def gate_spec(n):
        return pl.BlockSpec((tm, tn), lambda i, j: (i, (gate_col0 + n * d) // tn + j))

    return _call(
        functools.partial(_branch_kernel, precise=grp.precise), "branch_merge", (rows // tm, d // tn),
        [y_spec, y_spec, y_spec, gate_spec(0), gate_spec(1), gate_spec(2),
         pl.BlockSpec((None, N_BRANCH, bw, tn), lambda i, j: (l, 0, 0, j))],
        pl.BlockSpec((tm, tn), lambda i, j: (i, j)),
        jax.ShapeDtypeStruct((rows, d), grp.act), sem=("parallel", "arbitrary"),
    )(ya, yb, yc, proj, proj, proj, w_branch)


def _out_proj_kernel(m_ref, w_ref, x_ref, g_ref, o_ref, *, precise):
    o_ref[...] = x_ref[...] + g_ref[...] * _mm(m_ref[...], w_ref[...], precise)


def _out_proj(grp, l, merged, w_out, x):
    rows, d = x.shape
    tm, tn = grp.tm, 512
    return _call(
        functools.partial(_out_proj_kernel, precise=grp.precise), "out_proj", (rows // tm, d // tn),
        [pl.BlockSpec((tm, d), lambda i, j: (i, 0)),
         pl.BlockSpec((None, d, tn), lambda i, j: (l, 0, j)),
         pl.BlockSpec((tm, tn), lambda i, j: (i, j)),
         grp.mod_spec(l, 2, d, tn)],
        pl.BlockSpec((tm, tn), lambda i, j: (i, j)),
        jax.ShapeDtypeStruct((rows, d), F32), sem=("parallel", "arbitrary"),
    )(merged, w_out, x, grp.mods)


def _first_index_of_max(vals):
    m = functools.reduce(jnp.maximum, vals)
    idx = jnp.full(m.shape, len(vals) - 1, jnp.int32)
    for k in reversed(range(len(vals) - 1)):
        idx = jnp.where(vals[k] == m, k, idx)
    return m, idx


def _moe_prep_kernel(x_ref, g_ref, sh_ref, sc_ref, rw_ref, rb_ref, h_ref, r_ref):
    h = _modulated_norm(x_ref[...], g_ref[...], sc_ref[...], sh_ref[...])
    h_ref[...] = h.astype(h_ref.dtype)
    h_hi, h_lo = _split_hi_lo(h)
    w_hi, w_lo = _split_hi_lo(rw_ref[...])
    dot = functools.partial(jnp.dot, preferred_element_type=F32)
    logits = dot(h_hi, w_hi) + dot(h_lo, w_hi) + dot(h_hi, w_lo) + rb_ref[...]
    col = lambda k: logits[:, k:k + 1]
    gl = [col(k) for k in range(N_GROUPS)]
    g_max, g_idx = _first_index_of_max(gl)
    g_w = 1.0 / functools.reduce(jnp.add, [jnp.exp(v - g_max) for v in gl])
    e_in = []
    for j in range(EXPERTS_PER_GROUP):
        v = col(N_GROUPS + (N_GROUPS - 1) * EXPERTS_PER_GROUP + j)
        for g in reversed(range(N_GROUPS - 1)):
            v = jnp.where(g_idx == g, col(N_GROUPS + g * EXPERTS_PER_GROUP + j), v)
        e_in.append(v)
    m1, i1 = _first_index_of_max(e_in)
    rest = [jnp.where(i1 == j, -jnp.inf, e_in[j]) for j in range(EXPERTS_PER_GROUP)]
    m2, i2 = _first_index_of_max(rest)
    ratio = jnp.exp(m2 - m1)
    p1 = g_w / (1.0 + ratio)
    p2 = p1 * ratio
    lane = lax.broadcasted_iota(jnp.int32, r_ref.shape, 1)
    e_grp = lane // EXPERTS_PER_GROUP
    e_loc = lane % EXPERTS_PER_GROUP
    w = jnp.where(e_loc == i1, p1, 0.0) + jnp.where(e_loc == i2, p2, 0.0)
    r_ref[...] = jnp.where(e_grp == g_idx, w, 0.0)


def _moe_prep(grp, l, x, g, rw, rb):
    rows, d = x.shape
    tm = grp.tm
    return _call(
        _moe_prep_kernel, "moe_prep", (rows // tm, 1),
        [pl.BlockSpec((tm, d), lambda i, j: (i, 0)),
         pl.BlockSpec((None, 1, d), lambda i, j: (l, 0, 0)),
         grp.mod_spec(l, 3, d), grp.mod_spec(l, 4, d),
         pl.BlockSpec((d, LANE), lambda i, j: (0, 0)),
         pl.BlockSpec((1, LANE), lambda i, j: (0, 0))],
        [pl.BlockSpec((tm, d), lambda i, j: (i, 0)), pl.BlockSpec((tm, LANE), lambda i, j: (i, 0))],
        [jax.ShapeDtypeStruct((rows, d), grp.act), jax.ShapeDtypeStruct((rows, LANE), F32)],
        sem=("parallel", "arbitrary"),
    )(x, g, grp.mods, grp.mods, rw, rb)


def _moe_kernel(h_ref, r_ref, wg_ref, wu_ref, wd_ref, x_ref, g_ref, o_ref, *, precise):
    e = pl.program_id(1)

    @pl.when(e == 0)
    def _():
        o_ref[...] = jnp.zeros_like(o_ref)

    h = h_ref[...]
    a = _mm(h, wg_ref[...], precise)
    b = _mm(h, wu_ref[...], precise)
    lane = lax.broadcasted_iota(jnp.int32, r_ref.shape, 1)
    c = jnp.sum(jnp.where(lane == e, r_ref[...], 0.0), axis=1, keepdims=True)
    o_ref[...] += _mm(_silu(a) * b * c, wd_ref[...], precise)

    @pl.when(e == pl.num_programs(1) - 1)
    def _():
        o_ref[...] = x_ref[...] + g_ref[...] * o_ref[...]


def _moe(grp, l, h, route, w_gate, w_up, w_down, x):
    rows, d = x.shape
    n_e, _, f = w_gate.shape[1:]
    tm = min(grp.tm, 512)
    return _call(
        functools.partial(_moe_kernel, precise=grp.precise), "moe", (rows // tm, n_e),
        [pl.BlockSpec((tm, d), lambda i, e: (i, 0)),
         pl.BlockSpec((tm, LANE), lambda i, e: (i, 0)),
         pl.BlockSpec((None, None, d, f), lambda i, e: (l, e, 0, 0)),
         pl.BlockSpec((None, None, d, f), lambda i, e: (l, e, 0, 0)),
         pl.BlockSpec((None, None, f, d), lambda i, e: (l, e, 0, 0)),
         pl.BlockSpec((tm, d), lambda i, e: (i, 0)),
         grp.mod_spec(l, 5, d, tm=tm)],
        pl.BlockSpec((tm, d), lambda i, e: (i, 0)),
        jax.ShapeDtypeStruct((rows, d), F32), sem=("parallel", "arbitrary"),
    )(h, route, w_gate, w_up, w_down, x, grp.mods)


def _router_logits(h, rw, rb):
    h_hi, h_lo = _split_hi_lo(h)
    w_hi, w_lo = _split_hi_lo(rw)
    dot = functools.partial(jnp.dot, preferred_element_type=F32)
    return dot(h_hi, w_hi) + dot(h_lo, w_hi) + dot(h_hi, w_lo) + rb


def _top2_weights(e_in, g_w):
    m1, i1 = _first_index_of_max(e_in)
    rest = [jnp.where(i1 == j, -jnp.inf, e_in[j]) for j in range(len(e_in))]
    m2, i2 = _first_index_of_max(rest)
    ratio = jnp.exp(m2 - m1)
    p1 = g_w / (1.0 + ratio)
    return i1, p1, i2, p1 * ratio


def _moe_route_kernel(x_ref, g_ref, sh_ref, sc_ref, rw_ref, rb_ref, tri_ref, ht_ref, info_ref, cnt_ref, base_ref,
                      *, slab):
    @pl.when(pl.program_id(0) == 0)
    def _():
        base_ref[...] = jnp.zeros_like(base_ref)

    h = _modulated_norm(x_ref[...], g_ref[...], sc_ref[...], sh_ref[...])
    tm = h.shape[0]
    for s in range(slab):
        ht_ref[pl.ds(s, tm, stride=slab), :] = h[:, s * LANE:(s + 1) * LANE]
    logits = _router_logits(h, rw_ref[...], rb_ref[...])
    _, g_idx = _first_index_of_max([logits[:, k:k + 1] for k in range(N_GROUPS)])
    lane = lax.broadcasted_iota(jnp.int32, (tm, LANE), 1)
    onehot = jnp.where(lane == g_idx, 1.0, 0.0)
    before = jnp.dot(tri_ref[...], onehot.astype(BF16), preferred_element_type=F32)
    base = base_ref[0:1, :]
    rank = jnp.sum(jnp.where(lane == g_idx, before + base, 0.0), axis=1, keepdims=True)
    info_ref[...] = jnp.where(lane == 0, g_idx.astype(F32), jnp.where(lane == 1, rank, 0.0))
    total = base + before[tm - 1:tm, :] + onehot[tm - 1:tm, :]
    base_ref[...] = jnp.broadcast_to(total, base_ref.shape)
    cnt_ref[...] = jnp.broadcast_to(total, cnt_ref.shape)


def _moe_route(grp, l, x, g, rw, rb, tm):
    rows, d = x.shape
    slab = d // LANE
    r = jnp.arange(tm)
    tri = (r[:, None] > r[None, :]).astype(BF16)
    return _call(
        functools.partial(_moe_route_kernel, slab=slab), "moe_route", (rows // tm, 1),
        [pl.BlockSpec((tm, d), lambda i, j: (i, 0)),
         pl.BlockSpec((None, 1, d), lambda i, j: (l, 0, 0)),
         grp.mod_spec(l, 3, d, tm=tm), grp.mod_spec(l, 4, d, tm=tm),
         pl.BlockSpec((d, LANE), lambda i, j: (0, 0)),
         pl.BlockSpec((1, LANE), lambda i, j: (0, 0)),
         pl.BlockSpec((tm, tm), lambda i, j: (0, 0))],
        [pl.BlockSpec((tm * slab, LANE), lambda i, j: (i, 0)),
         pl.BlockSpec((tm, LANE), lambda i, j: (i, 0)),
         pl.BlockSpec((SUBLANE, LANE), lambda i, j: (0, 0))],
        [jax.ShapeDtypeStruct((rows * slab, LANE), F32), jax.ShapeDtypeStruct((rows, LANE), F32),
         jax.ShapeDtypeStruct((SUBLANE, LANE), F32)],
        scratch=[pltpu.VMEM((SUBLANE, LANE), F32)], sem=("arbitrary", "arbitrary"),
    )(x, g, grp.mods, grp.mods, rw, rb, tri)


def _token_copy(src_ref, src_tok, dst_ref, dst_tok, sem, slab):
    return pltpu.make_async_copy(
        src_ref.at[pl.ds(pl.multiple_of(src_tok * slab, slab), slab), :],
        dst_ref.at[pl.ds(pl.multiple_of(dst_tok * slab, slab), slab), :], sem)


def _dispatch_kernel(pos_ref, src_ref, dst_in_ref, dst_ref, sem, *, tb, slab):
    del dst_in_ref
    t0 = pl.program_id(0) * tb

    def issue(t, carry):
        _token_copy(src_ref, t0 + t, dst_ref, pos_ref[t0 + t], sem, slab).start()
        return carry

    def wait(t, carry):
        _token_copy(src_ref, 0, dst_ref, 0, sem, slab).wait()
        return carry

    lax.fori_loop(0, tb, issue, 0)
    lax.fori_loop(0, tb, wait, 0)


def _dispatch(pos, ht, n_dst_rows, slab):
    rows = pos.shape[0]
    tb = min(1024, rows)
    grid_spec = pltpu.PrefetchScalarGridSpec(
        num_scalar_prefetch=1, grid=(rows // tb,),
        in_specs=[pl.BlockSpec(memory_space=pl.ANY), pl.BlockSpec(memory_space=pl.ANY)],
        out_specs=pl.BlockSpec(memory_space=pl.ANY),
        scratch_shapes=[pltpu.SemaphoreType.DMA(())])
    return pl.pallas_call(
        functools.partial(_dispatch_kernel, tb=tb, slab=slab), grid_spec=grid_spec, name="moe_dispatch",
        out_shape=jax.ShapeDtypeStruct((n_dst_rows * slab, LANE), F32), input_output_aliases={2: 0},
        compiler_params=pltpu.CompilerParams(dimension_semantics=("arbitrary",), vmem_limit_bytes=VMEM_LIMIT),
    )(pos, ht, jnp.zeros((n_dst_rows * slab, LANE), F32))


def _moe_sorted_kernel(tg_ref, nu_ref, xs_ref, rw_ref, rb_ref, wg_ref, wu_ref, wd_ref, o_ref, h_s, comb_s, acc_s,
                       *, slab):
    i = pl.program_id(0)
    j = pl.program_id(1)
    last = pl.num_programs(1) - 1
    tm = h_s.shape[0]
    used = i < nu_ref[0]
    lane = lax.broadcasted_iota(jnp.int32, (tm, LANE), 1)

    @pl.when(used & (j == 0))
    def _():
        h = jnp.concatenate([xs_ref[pl.ds(s, tm, stride=slab), :] for s in range(slab)], axis=1)
        h_s[...] = h.astype(BF16)
        logits = _router_logits(h, rw_ref[...], rb_ref[...])
        g = tg_ref[i]
        pick = lambda k: jnp.sum(jnp.where(lane == k, logits, 0.0), axis=1, keepdims=True)
        gl = [logits[:, k:k + 1] for k in range(N_GROUPS)]
        g_max = functools.reduce(jnp.maximum, gl)
        g_w = jnp.exp(pick(g) - g_max) / functools.reduce(jnp.add, [jnp.exp(v - g_max) for v in gl])
        e_in = [pick(N_GROUPS + g * EXPERTS_PER_GROUP + k) for k in range(EXPERTS_PER_GROUP)]
        i1, p1, i2, p2 = _top2_weights(e_in, g_w)
        comb_s[...] = jnp.where(lane == i1, p1, 0.0) + jnp.where(lane == i2, p2, 0.0)
        acc_s[...] = jnp.zeros_like(acc_s)

    @pl.when(used)
    def _():
        h = h_s[...]
        a = jnp.dot(h, wg_ref[...].astype(BF16), preferred_element_type=F32)
        b = jnp.dot(h, wu_ref[...].astype(BF16), preferred_element_type=F32)
        c = jnp.sum(jnp.where(lane == j, comb_s[...], 0.0), axis=1, keepdims=True)
        acc_s[...] += _bdot(_silu(a) * b * c, wd_ref[...])

    @pl.when(used & (j == last))
    def _():
        y = acc_s[...]
        for s in range(slab):
            o_ref[pl.ds(s, tm, stride=slab), :] = y[:, s * LANE:(s + 1) * LANE]

    @pl.when(jnp.logical_not(used) & (j == last))
    def _():
        o_ref[...] = jnp.zeros_like(o_ref)


def _moe_sorted(l, tile_group, n_used, xs, rw, rb, w_gate, w_up, w_down, tm):
    n_e, d, f = w_gate.shape[1:]
    slab = d // LANE
    n_tiles = xs.shape[0] // (tm * slab)
    epg = EXPERTS_PER_GROUP

    def expert(i, j, tg, nu):
        return jnp.where(i < nu[0], tg[i] * epg + j, tg[nu[0] - 1] * epg + epg - 1)

    w_spec = lambda a, b: pl.BlockSpec((None, None, a, b), lambda i, j, tg, nu: (l, expert(i, j, tg, nu), 0, 0))
    return _call(
        functools.partial(_moe_sorted_kernel, slab=slab), "moe_sorted", (n_tiles, epg),
        [pl.BlockSpec((tm * slab, LANE), lambda i, j, tg, nu: (jnp.minimum(i, nu[0] - 1), 0)),
         pl.BlockSpec((d, LANE), lambda i, j, tg, nu: (0, 0)),
         pl.BlockSpec((1, LANE), lambda i, j, tg, nu: (0, 0)),
         w_spec(d, f), w_spec(d, f), w_spec(f, d)],
        pl.BlockSpec((tm * slab, LANE), lambda i, j, tg, nu: (i, 0)),
        jax.ShapeDtypeStruct(xs.shape, F32),
        scratch=[pltpu.VMEM((tm, d), BF16), pltpu.VMEM((tm, LANE), F32), pltpu.VMEM((tm, d), F32)],
        sem=("arbitrary", "arbitrary"), n_prefetch=2,
    )(tile_group, n_used, xs, rw, rb, w_gate, w_up, w_down)


def _collect_kernel(pos_ref, ys_ref, x_ref, g_ref, o_ref, buf, sem, *, slab):
    tm = x_ref.shape[0]
    t0 = pl.program_id(0) * tm

    def issue(t, carry):
        _token_copy(ys_ref, pos_ref[t0 + t], buf, t, sem, slab).start()
        return carry

    def wait(t, carry):
        _token_copy(ys_ref, 0, buf, 0, sem, slab).wait()
        return carry

    lax.fori_loop(0, tm, issue, 0)
    lax.fori_loop(0, tm, wait, 0)
    y = jnp.concatenate([buf[pl.ds(s, tm, stride=slab), :] for s in range(slab)], axis=1)
    o_ref[...] = x_ref[...] + g_ref[...] * y


def _collect(grp, l, pos, ys, x, tm):
    rows, d = x.shape
    slab = d // LANE
    return _call(
        functools.partial(_collect_kernel, slab=slab), "moe_collect", (rows // tm, 1),
        [pl.BlockSpec(memory_space=pl.ANY),
         pl.BlockSpec((tm, d), lambda i, j, pos: (i, 0)),
         grp.mod_spec(l, 5, d, tm=tm)],
        pl.BlockSpec((tm, d), lambda i, j, pos: (i, 0)),
        jax.ShapeDtypeStruct((rows, d), F32),
        scratch=[pltpu.VMEM((tm * slab, LANE), F32), pltpu.SemaphoreType.DMA(())],
        sem=("arbitrary", "arbitrary"), n_prefetch=1,
    )(pos, ys, x, grp.mods)


def _moe_by_group(grp, l, x, g, rw, rb, w_gate, w_up, w_down):
    rows, d = x.shape
    slab = d // LANE
    tm = min(512, rows)
    ht, info, cnt = _moe_route(grp, l, x, g, rw, rb, tm)
    g_idx = info[:, 0].astype(jnp.int32)
    rank = info[:, 1].astype(jnp.int32)
    counts = cnt[0, :N_GROUPS].astype(jnp.int32)
    tiles = (counts + tm - 1) // tm
    tile_end = jnp.cumsum(tiles)
    pos = (tile_end - tiles)[g_idx] * tm + rank
    n_tiles = rows // tm + N_GROUPS
    tile_group = jnp.sum(jnp.arange(n_tiles)[:, None] >= tile_end[None, :N_GROUPS - 1], axis=1).astype(jnp.int32)
    n_used = tile_end[N_GROUPS - 1:].astype(jnp.int32)
    xs = _dispatch(pos, ht, n_tiles * tm, slab)
    ys = _moe_sorted(l, tile_group, n_used, xs, rw, rb, w_gate, w_up, w_down, tm)
    return _collect(grp, l, pos, ys, x, tm)


def _final_norm_kernel(x_ref, g_ref, o_ref):
    x = x_ref[...]
    o_ref[...] = x * lax.rsqrt(jnp.mean(x * x, axis=-1, keepdims=True) + EPS) * g_ref[...]


def _final_norm(x, g, tm):
    rows, d = x.shape
    return _call(
        _final_norm_kernel, "final_norm", (rows // tm,),
        [pl.BlockSpec((tm, d), lambda i: (i, 0)), pl.BlockSpec((1, d), lambda i: (0, 0))],
        pl.BlockSpec((tm, d), lambda i: (i, 0)),
        jax.ShapeDtypeStruct((rows, d), F32), sem=("parallel",),
    )(x, g.reshape(1, d))


def kernel(x_prompt, x_sample, cache_k, cache_v, state_conv, state_lru, page_table, c_prompt, c_sample, ada_w, ada_b, norm_mix_g, norm_ffn_g, norm_final_g, w_in, sb_logit_bias, cm_norm_g, cm_w_s, cm_b_s, lru_conv_w, lru_conv_b, lru_w_a, lru_b_a, lru_w_x, lru_b_x, lru_lambda, w_branch, w_out, router_group_w, router_group_b, router_expert_w, router_expert_b, moe_w_gate, moe_w_up, moe_w_down):
    n_bp, t_p, d = x_prompt.shape
    n_bs, t_s, _ = x_sample.shape
    depth = w_in.shape[0]
    n_heads, dh = cache_k.shape[3], cache_k.shape[4]
    sbw = n_heads * dh
    cmw = cm_norm_g.shape[-1]
    lw = lru_lambda.shape[-1]
    chunk = cm_w_s.shape[-1]
    assert t_s >= CONV_W - 1 and t_s <= SUBLANE and t_p % chunk == 0
    col_q, col_k, col_v = 0, sbw, 2 * sbw
    col_cu, col_cv = 3 * sbw, 3 * sbw + cmw
    col_lx, col_lg = 3 * sbw + 2 * cmw, 3 * sbw + 2 * cmw + lw
    col_gates = 3 * sbw + 2 * cmw + 2 * lw
    assert cmw == sbw and lw == sbw

    n_c = n_bp + n_bs
    c_rows = -(-n_c // SUBLANE) * SUBLANE
    c_all = jnp.pad(jnp.concatenate([c_prompt, c_sample], axis=0), ((0, c_rows - n_c), (0, 0)))
    mods = _ada_mods(c_all, ada_w, ada_b)
    rows_p, rows_s = n_bp * t_p, n_bs * t_s
    tm_p = min(1024, t_p)
    grp_p = _Group(mods[:, :n_bp].reshape(depth, n_bp, 1, N_MODS * d), rows_p, tm_p, t_p, False, False)
    grp_s = _Group(jnp.repeat(mods[:, n_bp:n_c], t_s, axis=1), rows_s, rows_s, t_s, True, True)

    norm_mix = norm_mix_g.reshape(depth, 1, d)
    norm_ffn = norm_ffn_g.reshape(depth, 1, d)
    rw = jnp.pad(jnp.concatenate([router_group_w, router_expert_w], axis=-1),
                 ((0, 0), (0, 0), (0, LANE - N_GROUPS - N_EXPERTS)))
    rb = jnp.pad(jnp.concatenate([router_group_b, router_expert_b], axis=-1),
                 ((0, 0), (0, LANE - N_GROUPS - N_EXPERTS))).reshape(depth, 1, LANE)

    n_rep = rows_s // t_s
    cm_w_small = jnp.tile(cm_w_s[:, :, :t_s, :t_s], (1, 1, n_rep, n_rep))
    cm_b_small = jnp.tile(cm_b_s[:, :, :t_s], (1, 1, n_rep))[..., None]
    cm_b_full = cm_b_s[..., None]
    zero_hist = jnp.zeros((n_bp, SUBLANE, lw), F32)
    zero_h = jnp.zeros((n_bp, lw), F32)
    hist_s = jnp.pad(state_conv, ((0, 0), (0, 0), (SUBLANE - (CONV_W - 1), 0), (0, 0)))
    tt_p = min(256, t_p)
    tm_cm = min(512, t_p)

    def layer(l, grp, x, prompt):
        outs = _in_proj(grp, l, _norm_mod(grp, l, x, norm_mix), w_in, col_k, 2 * sbw)
        proj = outs[0]
        blk = lambda c: c // sbw
        if prompt:
            k_new = outs[1][:, :sbw].reshape(n_bp, t_p, n_heads, dh)
            v_new = outs[1][:, sbw:].reshape(n_bp, t_p, n_heads, dh)
            y_a = _sb_attention_prompt(proj, sb_logit_bias[l], n_bp, t_p, n_heads, dh)
            y_b, = _chunk_mix(proj, blk(col_cu), blk(col_cv), cm_norm_g[l], cm_w_s[l], cm_b_full[l],
                              tm_cm, chunk, chunk, False, False)
            vn = None
            y_c, tail, hs = _rglru(proj, blk(col_lx), blk(col_lg), n_bp, t_p, tt_p, zero_hist, zero_h,
                                   lru_conv_w[l], lru_conv_b[l], lru_w_a[l], lru_b_a[l], lru_w_x[l], lru_b_x[l],
                                   lru_lambda[l], False)
            t_last = t_p
        else:
            q3 = proj[:, col_q:col_q + sbw].reshape(n_bs, t_s, sbw)
            k3 = proj[:, col_k:col_k + sbw].reshape(n_bs, t_s, sbw)
            v3 = proj[:, col_v:col_v + sbw].reshape(n_bs, t_s, sbw)
            k_new = k3.reshape(n_bs, t_s, n_heads, dh)
            v_new = v3.reshape(n_bs, t_s, n_heads, dh)
            y_a = _sb_attention_sample(l, q3, k3, v3, cache_k, cache_v, page_table, sb_logit_bias[l])
            y_a = y_a.reshape(rows_s, sbw)
            y_b, vn = _chunk_mix(proj, blk(col_cu), blk(col_cv), cm_norm_g[l], cm_w_small[l], cm_b_small[l],
                                 rows_s, rows_s, t_s, True, True)
            vn = vn.reshape(n_bs, t_s, cmw)
            pad_t = lambda a: jnp.pad(a.reshape(n_bs, t_s, lw), ((0, 0), (0, SUBLANE - t_s), (0, 0))).reshape(
                n_bs * SUBLANE, lw)
            xg = jnp.concatenate([pad_t(proj[:, col_lx:col_lx + lw]), pad_t(proj[:, col_lg:col_lg + lw])], axis=1)
            y_c, tail, hs = _rglru(xg, 0, 1, n_bs, SUBLANE, SUBLANE, hist_s[l], state_lru[l],
                                   lru_conv_w[l], lru_conv_b[l], lru_w_a[l], lru_b_a[l], lru_w_x[l], lru_b_x[l],
                                   lru_lambda[l], True)
            y_c = y_c.reshape(n_bs, SUBLANE, lw)[:, :t_s].reshape(rows_s, lw)
            t_last = t_s
        end = (t_last - 1) % SUBLANE + 1
        conv_new = tail[:, end - (CONV_W - 1):end]
        h_new = hs[:, end - 1]
        merged = _branch_merge(grp, l, y_a, y_b, y_c, proj, col_gates, w_branch)
        x = _out_proj(grp, l, merged, w_out, x)
        if prompt:
            x = _moe_by_group(grp, l, x, norm_ffn, rw[l], rb[l], moe_w_gate, moe_w_up, moe_w_down)
        else:
            h, route = _moe_prep(grp, l, x, norm_ffn, rw[l], rb[l])
            x = _moe(grp, l, h, route, moe_w_gate, moe_w_up, moe_w_down, x)
        return x, (k_new, v_new, vn, conv_new, h_new)

    def trunk(grp, x, prompt):
        outs = []
        for l in range(depth):
            x, o = layer(l, grp, x, prompt)
            outs.append(o)
        y = _final_norm(x, norm_final_g, grp.tm)
        return y, [None if prompt and i == 2 else jnp.stack([o[i] for o in outs]) for i in range(5)]

    y_p, (k_p, v_p, _, conv_p, lru_p) = trunk(grp_p, x_prompt.reshape(rows_p, d), True)
    y_s, (k_s, v_s, cv_s, conv_s, lru_s) = trunk(grp_s, x_sample.reshape(rows_s, d), False)
    return (y_p.reshape(n_bp, t_p, d), y_s.reshape(n_bs, t_s, d), k_p, v_p, k_s, v_s, cv_s,
            conv_p, conv_s, lru_p, lru_s)
```

```python
import functools

import jax
import jax.numpy as jnp
from jax import lax
from jax.experimental import pallas as pl
from jax.experimental.pallas import tpu as pltpu

F32 = jnp.float32
BF16 = jnp.bfloat16

EPS = 1e-6
LRU_C = 8.0
N_MODS = 6
N_GROUPS = 4
EXPERTS_PER_GROUP = 4
N_EXPERTS = N_GROUPS * EXPERTS_PER_GROUP
CONV_W = 4
N_BRANCH = 3
LANE = 128
SUBLANE = 8
VMEM_LIMIT = 56 * 1024 * 1024


def _call(kernel, name, grid, in_specs, out_specs, out_shape, scratch=(), sem=None, n_prefetch=0):
    grid_spec = pltpu.PrefetchScalarGridSpec(
        num_scalar_prefetch=n_prefetch, grid=grid, in_specs=in_specs, out_specs=out_specs,
        scratch_shapes=list(scratch))
    return pl.pallas_call(
        kernel, grid_spec=grid_spec, out_shape=out_shape, name=name,
        compiler_params=pltpu.CompilerParams(dimension_semantics=sem, vmem_limit_bytes=VMEM_LIMIT))


def _bdot(a, b):
    return jnp.dot(a.astype(BF16), b.astype(BF16), preferred_element_type=F32)


def _split_hi_lo(x):
    hi = x.astype(BF16)
    lo = (x - hi.astype(F32)).astype(BF16)
    return hi, lo


def _hdot(a, b):
    a_hi, a_lo = _split_hi_lo(a.astype(F32))
    b_hi, b_lo = _split_hi_lo(b.astype(F32))
    dot = functools.partial(jnp.dot, preferred_element_type=F32)
    m = a.shape[0]
    if m % (2 * SUBLANE):
        return dot(a_hi, b_hi) + dot(a_lo, b_hi) + dot(a_hi, b_lo)
    both = dot(jnp.concatenate([a_hi, a_lo], axis=0), b_hi)
    return both[:m] + both[m:] + dot(a_hi, b_lo)


def _mm(a, b, precise):
    return _hdot(a, b) if precise else _bdot(a, b)


def _softplus_neg_abs(z):
    u = jnp.exp(-jnp.abs(z))
    return jnp.where(u < 1e-3, u * (1.0 - u * (0.5 - u * (1.0 / 3.0))), jnp.log(1.0 + u))


def _log_sigmoid_pair(z):
    t = _softplus_neg_abs(z)
    return jnp.minimum(z, 0.0) - t, -jnp.maximum(z, 0.0) - t


def _neg_expm1(y):
    small = -y * (1.0 + y * (0.5 + y * (1.0 / 6.0 + y * (1.0 / 24.0))))
    return jnp.where(y > -0.05, small, 1.0 - jnp.exp(y))


def _silu(x):
    return x * jax.nn.sigmoid(x)


def _gelu(x):
    return jax.nn.gelu(x, approximate=True)


def _ada_kernel(c_ref, w_ref, b_ref, o_ref):
    o_ref[...] = _hdot(_silu(c_ref[...]), w_ref[...]) + b_ref[...]


def _ada_mods(c_all, ada_w, ada_b):
    n_l, d, n = ada_w.shape
    r = c_all.shape[0]
    tn = 1024
    return _call(
        _ada_kernel, "ada_mods", (n_l, n // tn),
        [pl.BlockSpec((r, d), lambda l, j: (0, 0)),
         pl.BlockSpec((None, d, tn), lambda l, j: (l, 0, j)),
         pl.BlockSpec((None, 1, tn), lambda l, j: (l, 0, j))],
        pl.BlockSpec((None, r, tn), lambda l, j: (l, 0, j)),
        jax.ShapeDtypeStruct((n_l, r, n), F32), sem=("parallel", "parallel"),
    )(c_all, ada_w, ada_b.reshape(n_l, 1, n))


class _Group:
    def __init__(self, mods, rows, tm, rows_per_batch, per_token, precise):
        self.mods, self.rows, self.tm = mods, rows, tm
        self.rows_per_batch, self.per_token = rows_per_batch, per_token
        self.precise = precise
        self.act = F32 if precise else BF16

    def mod_spec(self, l, k, d, tn=None, tm=None):
        tn_ = d if tn is None else tn
        tm_ = self.tm if tm is None else tm
        nj = d // tn_

        def col(j):
            return k * nj + (0 if tn is None else j)

        if self.per_token:
            return pl.BlockSpec((None, tm_, tn_), lambda i, j, *_: (l, i, col(j)))
        tpb = self.rows_per_batch // tm_
        return pl.BlockSpec((None, None, 1, tn_), lambda i, j, *_: (l, i // tpb, 0, col(j)))


def _modulated_norm(x, g, sc, sh):
    y = x * lax.rsqrt(jnp.mean(x * x, axis=-1, keepdims=True) + EPS) * g
    return y * (1.0 + sc) + sh


def _norm_mod_kernel(x_ref, g_ref, sh_ref, sc_ref, o_ref):
    o_ref[...] = _modulated_norm(x_ref[...], g_ref[...], sc_ref[...], sh_ref[...]).astype(o_ref.dtype)


def _norm_mod(grp, l, x, g):
    rows, d = x.shape
    tm = min(grp.tm, 512)
    return _call(
        _norm_mod_kernel, "norm_mod", (rows // tm, 1),
        [pl.BlockSpec((tm, d), lambda i, j: (i, 0)),
         pl.BlockSpec((None, 1, d), lambda i, j: (l, 0, 0)),
         grp.mod_spec(l, 0, d, tm=tm), grp.mod_spec(l, 1, d, tm=tm)],
        pl.BlockSpec((tm, d), lambda i, j: (i, 0)),
        jax.ShapeDtypeStruct((rows, d), grp.act), sem=("parallel", "arbitrary"),
    )(x, g, grp.mods, grp.mods)


def _in_proj_kernel(h_ref, w_ref, o_ref, *kv_ref, precise, kv_lo, kv_hi):
    y = _mm(h_ref[...], w_ref[...], precise)
    o_ref[...] = y.astype(o_ref.dtype)
    if kv_ref:
        j = pl.program_id(1)

        @pl.when((j >= kv_lo) & (j < kv_hi))
        def _():
            kv_ref[0][...] = y


def _in_proj(grp, l, h, w_in, kv_col0, kv_cols):
    rows, d = h.shape
    n = w_in.shape[2]
    tm = min(2 * grp.tm, grp.rows_per_batch) if not grp.per_token else grp.tm
    tn = 512
    kv_lo, kv_hi = kv_col0 // tn, (kv_col0 + kv_cols) // tn
    out_specs = [pl.BlockSpec((tm, tn), lambda i, j: (i, j))]
    out_shape = [jax.ShapeDtypeStruct((rows, n), grp.act)]
    if not grp.precise:
        out_specs.append(pl.BlockSpec((tm, tn), lambda i, j: (i, jnp.clip(j - kv_lo, 0, kv_hi - kv_lo - 1))))
        out_shape.append(jax.ShapeDtypeStruct((rows, kv_cols), F32))
    return _call(
        functools.partial(_in_proj_kernel, precise=grp.precise, kv_lo=kv_lo, kv_hi=kv_hi), "in_proj",
        (rows // tm, n // tn),
        [pl.BlockSpec((tm, d), lambda i, j: (i, 0)),
         pl.BlockSpec((None, d, tn), lambda i, j: (l, 0, j))],
        out_specs, out_shape, sem=("parallel", "arbitrary"),
    )(h, w_in)


def _cumsum_matrix(n):
    j = jnp.arange(n)[:, None]
    s = jnp.arange(n)[None, :]
    return jnp.concatenate([(j > s).astype(BF16), jnp.ones((n, n), BF16)], axis=1)


def _log_sigmoid_pair_fast(z):
    t = jnp.log(1.0 + jnp.exp(-jnp.abs(z)))
    return jnp.minimum(z, 0.0) - t, -jnp.maximum(z, 0.0) - t


def _sb_prompt_kernel(bias_ref, q_ref, k_ref, v_ref, u_ref, o_ref, c_ref, acc_ref, *, tq, ck, scale):
    h = pl.program_id(1)
    qi = pl.program_id(2)
    bias = bias_ref[h]
    q = q_ref[...].astype(BF16)
    acc_ref[...] = jnp.zeros_like(acc_ref)
    c_ref[...] = jnp.zeros_like(c_ref)
    n_chunks = tq // ck
    contract_last = (((1,), (1,)), ((), ()))

    for ci in reversed(range(n_chunks)):
        r0 = ci * ck
        start = pl.multiple_of(qi * tq + r0, ck)
        kc = k_ref[pl.ds(start, ck), :].astype(BF16)
        vc = v_ref[pl.ds(start, ck), :].astype(BF16)
        z = lax.dot_general(q[r0:, :], kc, contract_last, preferred_element_type=F32) * scale + bias
        log_beta, log_keep = _log_sigmoid_pair_fast(z)
        t_pos = lax.broadcasted_iota(jnp.int32, z.shape, 0)
        s_pos = lax.broadcasted_iota(jnp.int32, z.shape, 1)
        mask = s_pos < t_pos
        log_keep = jnp.where(mask, log_keep, 0.0)
        cs = jnp.dot(log_keep.astype(BF16), u_ref[...], preferred_element_type=F32)
        c = c_ref[r0:, :]
        p = jnp.where(mask, jnp.exp(log_beta + cs[:, :ck] + c), 0.0)
        acc_ref[r0:, :] += jnp.dot(p.astype(BF16), vc, preferred_element_type=F32)
        c_ref[r0:, :] = c + cs[:, ck:]

    def body(i, carry):
        start = pl.multiple_of((qi - 1 - i) * tq, tq)
        kt = k_ref[pl.ds(start, tq), :].astype(BF16)
        vt = v_ref[pl.ds(start, tq), :].astype(BF16)
        z = lax.dot_general(q, kt, contract_last, preferred_element_type=F32) * scale + bias
        log_beta, log_keep = _log_sigmoid_pair_fast(z)
        lk = log_keep.astype(BF16)
        c = c_ref[...]
        ps = [None] * n_chunks
        for ci in reversed(range(n_chunks)):
            cols = slice(ci * ck, (ci + 1) * ck)
            cs = jnp.dot(lk[:, cols], u_ref[...], preferred_element_type=F32)
            ps[ci] = jnp.exp(log_beta[:, cols] + cs[:, :ck] + c).astype(BF16)
            c = c + cs[:, ck:]
        acc_ref[...] += jnp.dot(jnp.concatenate(ps, axis=1), vt, preferred_element_type=F32)
        c_ref[...] = c
        return carry

    lax.fori_loop(0, qi, body, 0)
    o_ref[...] = acc_ref[...].astype(o_ref.dtype)


def _sb_attention_prompt(proj, bias, n_batch, t, n_heads, dh):
    tq = min(512, t)
    ck = LANE
    nq = t // tq
    kernel = functools.partial(_sb_prompt_kernel, tq=tq, ck=ck, scale=dh ** -0.5)
    return _call(
        kernel, "sb_prompt", (n_batch, n_heads, nq),
        [pl.BlockSpec(memory_space=pltpu.SMEM),
         pl.BlockSpec((tq, dh), lambda b, h, i: (b * nq + i, h)),
         pl.BlockSpec((t, dh), lambda b, h, i: (b, n_heads + h)),
         pl.BlockSpec((t, dh), lambda b, h, i: (b, 2 * n_heads + h)),
         pl.BlockSpec((ck, 2 * ck), lambda b, h, i: (0, 0))],
        pl.BlockSpec((tq, dh), lambda b, h, i: (b * nq + i, h)),
        jax.ShapeDtypeStruct((n_batch * t, n_heads * dh), BF16),
        scratch=[pltpu.VMEM((tq, ck), F32), pltpu.VMEM((tq, dh), F32)],
        sem=("parallel", "parallel", "arbitrary"),
    )(bias, proj, proj, proj, _cumsum_matrix(ck))


def _sb_sample_kernel(pt_ref, qbd_ref, bias_ref, kn_ref, vn_ref, ut_ref, *rest, n_pp, page, n_heads, s_new, nhq,
                      scale):
    k_refs = rest[:n_pp]
    v_refs = rest[n_pp:2 * n_pp]
    o_ref, c_ref, acc_ref = rest[2 * n_pp:]
    j = pl.program_id(1)
    bias = bias_ref[...]

    def rows_by_key(ref):
        return jnp.concatenate(
            [ref[pl.ds(h, page, stride=n_heads), :].astype(BF16) for h in range(n_heads)], axis=1)

    def visit(k_list, v_list, masked):
        n = len(k_list)
        k = jnp.concatenate([rows_by_key(r) for r in k_list], axis=0)
        v = jnp.concatenate([rows_by_key(r) for r in v_list], axis=0)
        z = jnp.dot(k, qbd_ref[...], preferred_element_type=F32) * scale + bias
        log_beta, log_keep = _log_sigmoid_pair(z)
        if masked:
            s_pos = lax.broadcasted_iota(jnp.int32, z.shape, 0)
            q_pos = lax.broadcasted_iota(jnp.int32, z.shape, 1) % s_new
            mask = s_pos < q_pos
            log_keep = jnp.where(mask, log_keep, 0.0)
        hi, lo = _split_hi_lo(log_keep)
        c = c_ref[...]
        ps = [None] * n
        for i in reversed(range(n)):
            rows = slice(i * page, (i + 1) * page)
            cs = jnp.dot(ut_ref[...], jnp.concatenate([hi[rows], lo[rows]], axis=0), preferred_element_type=F32)
            ps[i] = jnp.exp(log_beta[rows] + cs + c)
            c = c + cs[0:1, :] + log_keep[i * page:i * page + 1, :]
        c_ref[...] = c
        p = ps[0] if n == 1 else jnp.concatenate(ps, axis=0)
        if masked:
            p = jnp.where(mask, p, 0.0)
        pt = jnp.transpose(p)[:nhq, :].astype(BF16)
        acc_ref[...] += jnp.dot(pt, v, preferred_element_type=F32)

    @pl.when(j == 0)
    def _():
        c_ref[...] = jnp.zeros_like(c_ref)
        acc_ref[...] = jnp.zeros_like(acc_ref)
        visit([kn_ref], [vn_ref], True)

    visit(list(k_refs)[::-1], list(v_refs)[::-1], False)

    @pl.when(j == pl.num_programs(1) - 1)
    def _():
        o_ref[...] = acc_ref[...]


def _sb_attention_sample(l, q, k_new, v_new, cache_k, cache_v, page_table, bias):
    n_b, s_new, width = q.shape
    n_l, n_phys, page, n_heads, dh = cache_k.shape
    n_pages = page_table.shape[1]
    nhq = n_heads * s_new
    n_pp = 8 if n_pages % 8 == 0 else 1
    ck = cache_k.reshape(n_l, n_phys, page * n_heads, dh)
    cv = cache_v.reshape(n_l, n_phys, page * n_heads, dh)
    q4 = q.reshape(n_b, s_new, n_heads, dh)
    qbd = jnp.einsum('bqhd,hg->bhdgq', q4, jnp.eye(n_heads, dtype=q.dtype)).reshape(n_b, width, nhq)
    qbd = jnp.pad(qbd, ((0, 0), (0, 0), (0, LANE - nhq))).astype(BF16)
    bias_l = jnp.pad(jnp.repeat(bias, s_new), (0, LANE - nhq)).reshape(1, LANE)
    kn = jnp.pad(k_new, ((0, 0), (0, page - s_new), (0, 0))).reshape(n_b, page * n_heads, dh)
    vn = jnp.pad(v_new, ((0, 0), (0, page - s_new), (0, 0))).reshape(n_b, page * n_heads, dh)
    jj = jnp.arange(page)[None, :]
    ss = jnp.arange(page)[:, None]
    ut = (jj > ss).astype(BF16)
    ut2 = jnp.concatenate([ut, ut], axis=1)

    def page_spec(i):
        return pl.BlockSpec((None, None, page * n_heads, dh),
                            lambda b, j, pt: (l, pt[b, n_pages - 1 - (j * n_pp + i)], 0, 0))

    kernel = functools.partial(_sb_sample_kernel, n_pp=n_pp, page=page, n_heads=n_heads, s_new=s_new, nhq=nhq,
                               scale=dh ** -0.5)
    out = _call(
        kernel, "sb_sample", (n_b, n_pages // n_pp),
        [pl.BlockSpec((None, width, LANE), lambda b, j, pt: (b, 0, 0)),
         pl.BlockSpec((1, LANE), lambda b, j, pt: (0, 0)),
         pl.BlockSpec((None, page * n_heads, dh), lambda b, j, pt: (b, 0, 0)),
         pl.BlockSpec((None, page * n_heads, dh), lambda b, j, pt: (b, 0, 0)),
         pl.BlockSpec((page, 2 * page), lambda b, j, pt: (0, 0))]
        + [page_spec(i) for i in range(n_pp)] + [page_spec(i) for i in range(n_pp)],
        pl.BlockSpec((None, nhq, width), lambda b, j, pt: (b, 0, 0)),
        jax.ShapeDtypeStruct((n_b, nhq, width), F32),
        scratch=[pltpu.VMEM((1, LANE), F32), pltpu.VMEM((nhq, width), F32)],
        sem=("parallel", "arbitrary"), n_prefetch=1,
    )(page_table, qbd, bias_l, kn, vn, ut2, *([ck] * n_pp), *([cv] * n_pp))
    o5 = out.reshape(n_b, n_heads, s_new, n_heads, dh)
    hh = jnp.arange(n_heads)
    diag = o5[:, hh, :, hh, :]
    return jnp.transpose(diag, (1, 2, 0, 3)).reshape(n_b, s_new, width)


def _chunk_mix_kernel(u_ref, v_ref, g_ref, w_ref, b_ref, y_ref, *vn_out, lc, period, groups, precise):
    u = _gelu(u_ref[...].astype(F32))
    v = _gelu(v_ref[...].astype(F32))
    mu = jnp.mean(v, axis=-1, keepdims=True)
    var = jnp.mean(jnp.square(v - mu), axis=-1, keepdims=True)
    vn = (v - mu) * lax.rsqrt(var + EPS) * g_ref[...]
    if vn_out:
        vn_out[0][...] = vn
    vb = vn if precise else vn.astype(BF16)
    tm, width = u.shape
    gd = width // groups
    n_ch = tm // lc
    r = lax.broadcasted_iota(jnp.int32, (lc, lc), 0)
    c = lax.broadcasted_iota(jnp.int32, (lc, lc), 1)
    mask = c <= r if period == lc else (r // period == c // period) & (c <= r)
    for g in range(groups):
        w = jnp.where(mask, w_ref[g], 0.0)
        rhs = jnp.concatenate([vb[ch * lc:(ch + 1) * lc, g * gd:(g + 1) * gd] for ch in range(n_ch)], axis=1)
        mixed = _mm(w, rhs, precise) + b_ref[g]
        for ch in range(n_ch):
            rows = slice(ch * lc, (ch + 1) * lc)
            cols = slice(g * gd, (g + 1) * gd)
            y_ref[rows, cols] = (u[rows, cols] * mixed[:, ch * gd:(ch + 1) * gd]).astype(y_ref.dtype)


def _chunk_mix(proj, col_u, col_v, g, w, b, tm, lc, period, emit_vn, precise):
    rows = proj.shape[0]
    groups = w.shape[0]
    width = g.shape[-1]
    kernel = functools.partial(_chunk_mix_kernel, lc=lc, period=period, groups=groups, precise=precise)
    out_specs = [pl.BlockSpec((tm, width), lambda i: (i, 0))]
    out_shape = [jax.ShapeDtypeStruct((rows, width), F32 if precise else BF16)]
    if emit_vn:
        out_specs.append(pl.BlockSpec((tm, width), lambda i: (i, 0)))
        out_shape.append(jax.ShapeDtypeStruct((rows, width), F32))
    return _call(
        kernel, "chunk_mix", (rows // tm,),
        [pl.BlockSpec((tm, width), lambda i: (i, col_u)),
         pl.BlockSpec((tm, width), lambda i: (i, col_v)),
         pl.BlockSpec((1, width), lambda i: (0, 0)),
         pl.BlockSpec((groups, lc, lc), lambda i: (0, 0, 0)),
         pl.BlockSpec((groups, lc, 1), lambda i: (0, 0, 0))],
        out_specs, out_shape, sem=("parallel",),
    )(proj, proj, g.reshape(1, width), w, b)


def _shift_rows(x, d, fill, row):
    tt = x.shape[0]
    if d % SUBLANE == 0:
        return jnp.concatenate([jnp.full((d,) + x.shape[1:], fill, x.dtype), x[:tt - d]], axis=0)
    return jnp.where(row < d, fill, pltpu.roll(x, d, 0))


def _rglru_kernel(x_ref, gate_ref, hist_ref, h0_ref, cw_ref, cb_ref, wa_ref, ba_ref, wx_ref, bx_ref, lam_ref,
                  y_ref, tail_ref, hs_ref, hist_s, h_s, *, precise):
    t = pl.program_id(1)

    @pl.when(t == 0)
    def _():
        hist_s[...] = hist_ref[...]
        h_s[...] = jnp.broadcast_to(h0_ref[...], h_s.shape)

    x = x_ref[...].astype(F32)
    tt, width = x.shape
    row = lax.broadcasted_iota(jnp.int32, (tt, width), 0)
    row8 = lax.broadcasted_iota(jnp.int32, (SUBLANE, width), 0)
    hist = hist_s[...]
    cw = cw_ref[...]
    xc = cb_ref[...] + x * cw[CONV_W - 1:CONV_W, :]
    for s in range(1, CONV_W):
        xr = pltpu.roll(x, s, 0)
        top = jnp.where(row8 < s, pltpu.roll(hist, s, 0), xr[:SUBLANE])
        xs = top if tt == SUBLANE else jnp.concatenate([top, xr[SUBLANE:]], axis=0)
        xc = xc + xs * cw[CONV_W - 1 - s:CONV_W - s, :]

    n_blk = wa_ref.shape[0]
    bd = width // n_blk
    ra, ri = [], []
    for n in range(n_blk):
        xb = xc[:, n * bd:(n + 1) * bd]
        ra.append(_mm(xb, wa_ref[n], precise))
        ri.append(_mm(xb, wx_ref[n], precise))
    r_gate = jax.nn.sigmoid(jnp.concatenate(ra, axis=1) + ba_ref[...])
    i_gate = jax.nn.sigmoid(jnp.concatenate(ri, axis=1) + bx_ref[...])
    lam = lam_ref[...]
    log_sig_lam = jnp.minimum(lam, 0.0) - _softplus_neg_abs(lam)
    log_a = LRU_C * r_gate * log_sig_lam
    a = jnp.exp(log_a)
    b = jnp.sqrt(_neg_expm1(2.0 * log_a)) * (i_gate * xc)

    d = 1
    while d < tt:
        b = a * _shift_rows(b, d, 0.0, row) + b
        a = a * _shift_rows(a, d, 1.0, row)
        d *= 2
    hs = a * h_s[SUBLANE - 1:SUBLANE, :] + b

    y_ref[...] = (hs * _gelu(gate_ref[...].astype(F32))).astype(y_ref.dtype)
    hist_s[...] = x[tt - SUBLANE:, :]
    h_s[...] = hs[tt - SUBLANE:, :]
    tail_ref[...] = x[tt - SUBLANE:, :]
    hs_ref[...] = hs[tt - SUBLANE:, :]


def _rglru(xsrc, col_x, col_g, n_batch, t, tt, hist8, h0, cw, cb, wa, ba, wx, bx, lam, precise):
    width = lam.shape[-1]
    nt = t // tt
    row2 = lambda a: a.reshape(1, width)
    full = lambda shape: pl.BlockSpec(shape, lambda b, i: (0,) * len(shape))
    return _call(
        functools.partial(_rglru_kernel, precise=precise), "rglru", (n_batch, nt),
        [pl.BlockSpec((tt, width), lambda b, i: (b * nt + i, col_x)),
         pl.BlockSpec((tt, width), lambda b, i: (b * nt + i, col_g)),
         pl.BlockSpec((None, SUBLANE, width), lambda b, i: (b, 0, 0)),
         pl.BlockSpec((None, 1, width), lambda b, i: (b, 0, 0)),
         full((CONV_W, width)), full((1, width)),
         full(wa.shape), full((1, width)), full(wx.shape), full((1, width)), full((1, width))],
        [pl.BlockSpec((tt, width), lambda b, i: (b * nt + i, 0)),
         pl.BlockSpec((None, SUBLANE, width), lambda b, i: (b, 0, 0)),
         pl.BlockSpec((None, SUBLANE, width), lambda b, i: (b, 0, 0))],
        [jax.ShapeDtypeStruct((n_batch * t, width), F32 if precise else BF16),
         jax.ShapeDtypeStruct((n_batch, SUBLANE, width), F32),
         jax.ShapeDtypeStruct((n_batch, SUBLANE, width), F32)],
        scratch=[pltpu.VMEM((SUBLANE, width), F32), pltpu.VMEM((SUBLANE, width), F32)],
        sem=("parallel", "arbitrary"),
    )(xsrc, xsrc, hist8, h0.reshape(n_batch, 1, width), cw, row2(cb), wa, row2(ba), wx, row2(bx), row2(lam))


def _branch_kernel(ya_ref, yb_ref, yc_ref, ga_ref, gb_ref, gc_ref, w_ref, o_ref, *, precise):
    acc = None
    for n, (y_ref, g_ref) in enumerate(((ya_ref, ga_ref), (yb_ref, gb_ref), (yc_ref, gc_ref))):
        term = jax.nn.sigmoid(g_ref[...].astype(F32)) * _mm(y_ref[...], w_ref[n], precise)
        acc = term if acc is None else acc + term
    o_ref[...] = acc.astype(o_ref.dtype)


def _branch_merge(grp, l, ya, yb, yc, proj, gate_col0, w_branch):
    rows, bw = ya.shape
    d = w_branch.shape[-1]
    tm, tn = grp.tm, 512
    y_spec = pl.BlockSpec((tm, bw), lambda i, j: (i, 0))

    def gate_spec(n):
        return pl.BlockSpec((tm, tn), lambda i, j: (i, (gate_col0 + n * d) // tn + j))

    return _call(
        functools.partial(_branch_kernel, precise=grp.precise), "branch_merge", (rows // tm, d // tn),
        [y_spec, y_spec, y_spec, gate_spec(0), gate_spec(1), gate_spec(2),
         pl.BlockSpec((None, N_BRANCH, bw, tn), lambda i, j: (l, 0, 0, j))],
        pl.BlockSpec((tm, tn), lambda i, j: (i, j)),
        jax.ShapeDtypeStruct((rows, d), grp.act), sem=("parallel", "arbitrary"),
    )(ya, yb, yc, proj, proj, proj, w_branch)


def _out_proj_kernel(m_ref, w_ref, x_ref, g_ref, o_ref, *, precise):
    o_ref[...] = x_ref[...] + g_ref[...] * _mm(m_ref[...], w_ref[...], precise)


def _out_proj(grp, l, merged, w_out, x):
    rows, d = x.shape
    tm, tn = grp.tm, 512
    return _call(
        functools.partial(_out_proj_kernel, precise=grp.precise), "out_proj", (rows // tm, d // tn),
        [pl.BlockSpec((tm, d), lambda i, j: (i, 0)),
         pl.BlockSpec((None, d, tn), lambda i, j: (l, 0, j)),
         pl.BlockSpec((tm, tn), lambda i, j: (i, j)),
         grp.mod_spec(l, 2, d, tn)],
        pl.BlockSpec((tm, tn), lambda i, j: (i, j)),
        jax.ShapeDtypeStruct((rows, d), F32), sem=("parallel", "arbitrary"),
    )(merged, w_out, x, grp.mods)


def _first_index_of_max(vals):
    m = functools.reduce(jnp.maximum, vals)
    idx = jnp.full(m.shape, len(vals) - 1, jnp.int32)
    for k in reversed(range(len(vals) - 1)):
        idx = jnp.where(vals[k] == m, k, idx)
    return m, idx


def _moe_prep_kernel(x_ref, g_ref, sh_ref, sc_ref, rw_ref, rb_ref, h_ref, r_ref):
    h = _modulated_norm(x_ref[...], g_ref[...], sc_ref[...], sh_ref[...])
    h_ref[...] = h.astype(h_ref.dtype)
    h_hi, h_lo = _split_hi_lo(h)
    w_hi, w_lo = _split_hi_lo(rw_ref[...])
    dot = functools.partial(jnp.dot, preferred_element_type=F32)
    logits = dot(h_hi, w_hi) + dot(h_lo, w_hi) + dot(h_hi, w_lo) + rb_ref[...]
    col = lambda k: logits[:, k:k + 1]
    gl = [col(k) for k in range(N_GROUPS)]
    g_max, g_idx = _first_index_of_max(gl)
    g_w = 1.0 / functools.reduce(jnp.add, [jnp.exp(v - g_max) for v in gl])
    e_in = []
    for j in range(EXPERTS_PER_GROUP):
        v = col(N_GROUPS + (N_GROUPS - 1) * EXPERTS_PER_GROUP + j)
        for g in reversed(range(N_GROUPS - 1)):
            v = jnp.where(g_idx == g, col(N_GROUPS + g * EXPERTS_PER_GROUP + j), v)
        e_in.append(v)
    m1, i1 = _first_index_of_max(e_in)
    rest = [jnp.where(i1 == j, -jnp.inf, e_in[j]) for j in range(EXPERTS_PER_GROUP)]
    m2, i2 = _first_index_of_max(rest)
    ratio = jnp.exp(m2 - m1)
    p1 = g_w / (1.0 + ratio)
    p2 = p1 * ratio
    lane = lax.broadcasted_iota(jnp.int32, r_ref.shape, 1)
    e_grp = lane // EXPERTS_PER_GROUP
    e_loc = lane % EXPERTS_PER_GROUP
    w = jnp.where(e_loc == i1, p1, 0.0) + jnp.where(e_loc == i2, p2, 0.0)
    r_ref[...] = jnp.where(e_grp == g_idx, w, 0.0)


def _moe_prep(grp, l, x, g, rw, rb):
    rows, d = x.shape
    tm = grp.tm
    return _call(
        _moe_prep_kernel, "moe_prep", (rows // tm, 1),
        [pl.BlockSpec((tm, d), lambda i, j: (i, 0)),
         pl.BlockSpec((None, 1, d), lambda i, j: (l, 0, 0)),
         grp.mod_spec(l, 3, d), grp.mod_spec(l, 4, d),
         pl.BlockSpec((d, LANE), lambda i, j: (0, 0)),
         pl.BlockSpec((1, LANE), lambda i, j: (0, 0))],
        [pl.BlockSpec((tm, d), lambda i, j: (i, 0)), pl.BlockSpec((tm, LANE), lambda i, j: (i, 0))],
        [jax.ShapeDtypeStruct((rows, d), grp.act), jax.ShapeDtypeStruct((rows, LANE), F32)],
        sem=("parallel", "arbitrary"),
    )(x, g, grp.mods, grp.mods, rw, rb)


def _moe_kernel(h_ref, r_ref, wg_ref, wu_ref, wd_ref, x_ref, g_ref, o_ref, *, precise):
    e = pl.program_id(1)

    @pl.when(e == 0)
    def _():
        o_ref[...] = jnp.zeros_like(o_ref)

    h = h_ref[...]
    a = _mm(h, wg_ref[...], precise)
    b = _mm(h, wu_ref[...], precise)
    lane = lax.broadcasted_iota(jnp.int32, r_ref.shape, 1)
    c = jnp.sum(jnp.where(lane == e, r_ref[...], 0.0), axis=1, keepdims=True)
    o_ref[...] += _mm(_silu(a) * b * c, wd_ref[...], precise)

    @pl.when(e == pl.num_programs(1) - 1)
    def _():
        o_ref[...] = x_ref[...] + g_ref[...] * o_ref[...]


def _moe(grp, l, h, route, w_gate, w_up, w_down, x):
    rows, d = x.shape
    n_e, _, f = w_gate.shape[1:]
    tm = min(grp.tm, 512)
    return _call(
        functools.partial(_moe_kernel, precise=grp.precise), "moe", (rows // tm, n_e),
        [pl.BlockSpec((tm, d), lambda i, e: (i, 0)),
         pl.BlockSpec((tm, LANE), lambda i, e: (i, 0)),
         pl.BlockSpec((None, None, d, f), lambda i, e: (l, e, 0, 0)),
         pl.BlockSpec((None, None, d, f), lambda i, e: (l, e, 0, 0)),
         pl.BlockSpec((None, None, f, d), lambda i, e: (l, e, 0, 0)),
         pl.BlockSpec((tm, d), lambda i, e: (i, 0)),
         grp.mod_spec(l, 5, d, tm=tm)],
        pl.BlockSpec((tm, d), lambda i, e: (i, 0)),
        jax.ShapeDtypeStruct((rows, d), F32), sem=("parallel", "arbitrary"),
    )(h, route, w_gate, w_up, w_down, x, grp.mods)


def _router_logits(h, rw, rb):
    h_hi, h_lo = _split_hi_lo(h)
    w_hi, w_lo = _split_hi_lo(rw)
    dot = functools.partial(jnp.dot, preferred_element_type=F32)
    return dot(h_hi, w_hi) + dot(h_lo, w_hi) + dot(h_hi, w_lo) + rb


def _top2_weights(e_in, g_w):
    m1, i1 = _first_index_of_max(e_in)
    rest = [jnp.where(i1 == j, -jnp.inf, e_in[j]) for j in range(len(e_in))]
    m2, i2 = _first_index_of_max(rest)
    ratio = jnp.exp(m2 - m1)
    p1 = g_w / (1.0 + ratio)
    return i1, p1, i2, p1 * ratio


def _moe_route_kernel(x_ref, g_ref, sh_ref, sc_ref, rw_ref, rb_ref, tri_ref, ht_ref, info_ref, cnt_ref, base_ref,
                      *, slab):
    @pl.when(pl.program_id(0) == 0)
    def _():
        base_ref[...] = jnp.zeros_like(base_ref)

    h = _modulated_norm(x_ref[...], g_ref[...], sc_ref[...], sh_ref[...])
    tm = h.shape[0]
    for s in range(slab):
        ht_ref[pl.ds(s, tm, stride=slab), :] = h[:, s * LANE:(s + 1) * LANE]
    logits = _router_logits(h, rw_ref[...], rb_ref[...])
    _, g_idx = _first_index_of_max([logits[:, k:k + 1] for k in range(N_GROUPS)])
    lane = lax.broadcasted_iota(jnp.int32, (tm, LANE), 1)
    onehot = jnp.where(lane == g_idx, 1.0, 0.0)
    before = jnp.dot(tri_ref[...], onehot.astype(BF16), preferred_element_type=F32)
    base = base_ref[0:1, :]
    rank = jnp.sum(jnp.where(lane == g_idx, before + base, 0.0), axis=1, keepdims=True)
    info_ref[...] = jnp.where(lane == 0, g_idx.astype(F32), jnp.where(lane == 1, rank, 0.0))
    total = base + before[tm - 1:tm, :] + onehot[tm - 1:tm, :]
    base_ref[...] = jnp.broadcast_to(total, base_ref.shape)
    cnt_ref[...] = jnp.broadcast_to(total, cnt_ref.shape)


def _moe_route(grp, l, x, g, rw, rb, tm):
    rows, d = x.shape
    slab = d // LANE
    r = jnp.arange(tm)
    tri = (r[:, None] > r[None, :]).astype(BF16)
    return _call(
        functools.partial(_moe_route_kernel, slab=slab), "moe_route", (rows // tm, 1),
        [pl.BlockSpec((tm, d), lambda i, j: (i, 0)),
         pl.BlockSpec((None, 1, d), lambda i, j: (l, 0, 0)),
         grp.mod_spec(l, 3, d, tm=tm), grp.mod_spec(l, 4, d, tm=tm),
         pl.BlockSpec((d, LANE), lambda i, j: (0, 0)),
         pl.BlockSpec((1, LANE), lambda i, j: (0, 0)),
         pl.BlockSpec((tm, tm), lambda i, j: (0, 0))],
        [pl.BlockSpec((tm * slab, LANE), lambda i, j: (i, 0)),
         pl.BlockSpec((tm, LANE), lambda i, j: (i, 0)),
         pl.BlockSpec((SUBLANE, LANE), lambda i, j: (0, 0))],
        [jax.ShapeDtypeStruct((rows * slab, LANE), F32), jax.ShapeDtypeStruct((rows, LANE), F32),
         jax.ShapeDtypeStruct((SUBLANE, LANE), F32)],
        scratch=[pltpu.VMEM((SUBLANE, LANE), F32)], sem=("arbitrary", "arbitrary"),
    )(x, g, grp.mods, grp.mods, rw, rb, tri)


def _token_copy(src_ref, src_tok, dst_ref, dst_tok, sem, slab):
    return pltpu.make_async_copy(
        src_ref.at[pl.ds(pl.multiple_of(src_tok * slab, slab), slab), :],
        dst_ref.at[pl.ds(pl.multiple_of(dst_tok * slab, slab), slab), :], sem)


def _dispatch_kernel(pos_ref, src_ref, dst_in_ref, dst_ref, sem, *, tb, slab):
    del dst_in_ref
    t0 = pl.program_id(0) * tb

    def issue(t, carry):
        _token_copy(src_ref, t, dst_ref, pos_ref[t0 + t], sem, slab).start()
        return carry

    def wait(t, carry):
        _token_copy(src_ref, 0, dst_ref, 0, sem, slab).wait()
        return carry

    lax.fori_loop(0, tb, issue, 0)
    lax.fori_loop(0, tb, wait, 0)


def _dispatch(pos, ht, n_dst_rows, slab):
    rows = pos.shape[0]
    tb = min(512, rows)
    grid_spec = pltpu.PrefetchScalarGridSpec(
        num_scalar_prefetch=1, grid=(rows // tb,),
        in_specs=[pl.BlockSpec((tb * slab, LANE), lambda i, pos: (i, 0)), pl.BlockSpec(memory_space=pl.ANY)],
        out_specs=pl.BlockSpec(memory_space=pl.ANY),
        scratch_shapes=[pltpu.SemaphoreType.DMA(())])
    return pl.pallas_call(
        functools.partial(_dispatch_kernel, tb=tb, slab=slab), grid_spec=grid_spec, name="moe_dispatch",
        out_shape=jax.ShapeDtypeStruct((n_dst_rows * slab, LANE), F32), input_output_aliases={2: 0},
        compiler_params=pltpu.CompilerParams(dimension_semantics=("arbitrary",), vmem_limit_bytes=VMEM_LIMIT),
    )(pos, ht, jnp.zeros((n_dst_rows * slab, LANE), F32))


def _moe_sorted_kernel(tg_ref, nu_ref, xs_ref, rw_ref, rb_ref, wg_ref, wu_ref, wd_ref, o_ref, h_s, comb_s, acc_s,
                       *, slab):
    i = pl.program_id(0)
    j = pl.program_id(1)
    last = pl.num_programs(1) - 1
    tm = h_s.shape[0]
    used = i < nu_ref[0]
    lane = lax.broadcasted_iota(jnp.int32, (tm, LANE), 1)

    @pl.when(used & (j == 0))
    def _():
        h = jnp.concatenate([xs_ref[pl.ds(s, tm, stride=slab), :] for s in range(slab)], axis=1)
        h_s[...] = h.astype(BF16)
        logits = _router_logits(h, rw_ref[...], rb_ref[...])
        g = tg_ref[i]
        pick = lambda k: jnp.sum(jnp.where(lane == k, logits, 0.0), axis=1, keepdims=True)
        gl = [logits[:, k:k + 1] for k in range(N_GROUPS)]
        g_max = functools.reduce(jnp.maximum, gl)
        g_w = jnp.exp(pick(g) - g_max) / functools.reduce(jnp.add, [jnp.exp(v - g_max) for v in gl])
        e_in = [pick(N_GROUPS + g * EXPERTS_PER_GROUP + k) for k in range(EXPERTS_PER_GROUP)]
        i1, p1, i2, p2 = _top2_weights(e_in, g_w)
        comb_s[...] = jnp.where(lane == i1, p1, 0.0) + jnp.where(lane == i2, p2, 0.0)
        acc_s[...] = jnp.zeros_like(acc_s)

    @pl.when(used)
    def _():
        h = h_s[...]
        a = jnp.dot(h, wg_ref[...].astype(BF16), preferred_element_type=F32)
        b = jnp.dot(h, wu_ref[...].astype(BF16), preferred_element_type=F32)
        c = jnp.sum(jnp.where(lane == j, comb_s[...], 0.0), axis=1, keepdims=True)
        acc_s[...] += _bdot(_silu(a) * b * c, wd_ref[...])

    @pl.when(used & (j == last))
    def _():
        y = acc_s[...]
        for s in range(slab):
            o_ref[pl.ds(s, tm, stride=slab), :] = y[:, s * LANE:(s + 1) * LANE]

    @pl.when(jnp.logical_not(used) & (j == last))
    def _():
        o_ref[...] = jnp.zeros_like(o_ref)


def _moe_sorted(l, tile_group, n_used, xs, rw, rb, w_gate, w_up, w_down, tm):
    n_e, d, f = w_gate.shape[1:]
    slab = d // LANE
    n_tiles = xs.shape[0] // (tm * slab)
    epg = EXPERTS_PER_GROUP

    def expert(i, j, tg, nu):
        return jnp.where(i < nu[0], tg[i] * epg + j, tg[nu[0] - 1] * epg + epg - 1)

    w_spec = lambda a, b: pl.BlockSpec((None, None, a, b), lambda i, j, tg, nu: (l, expert(i, j, tg, nu), 0, 0))
    return _call(
        functools.partial(_moe_sorted_kernel, slab=slab), "moe_sorted", (n_tiles, epg),
        [pl.BlockSpec((tm * slab, LANE), lambda i, j, tg, nu: (jnp.minimum(i, nu[0] - 1), 0)),
         pl.BlockSpec((d, LANE), lambda i, j, tg, nu: (0, 0)),
         pl.BlockSpec((1, LANE), lambda i, j, tg, nu: (0, 0)),
         w_spec(d, f), w_spec(d, f), w_spec(f, d)],
        pl.BlockSpec((tm * slab, LANE), lambda i, j, tg, nu: (i, 0)),
        jax.ShapeDtypeStruct(xs.shape, F32),
        scratch=[pltpu.VMEM((tm, d), BF16), pltpu.VMEM((tm, LANE), F32), pltpu.VMEM((tm, d), F32)],
        sem=("arbitrary", "arbitrary"), n_prefetch=2,
    )(tile_group, n_used, xs, rw, rb, w_gate, w_up, w_down)


def _collect_kernel(pos_ref, ys_ref, x_ref, g_ref, o_ref, buf, sem, *, slab):
    tm = x_ref.shape[0]
    t0 = pl.program_id(0) * tm

    def issue(t, carry):
        _token_copy(ys_ref, pos_ref[t0 + t], buf, t, sem, slab).start()
        return carry

    def wait(t, carry):
        _token_copy(ys_ref, 0, buf, 0, sem, slab).wait()
        return carry

    lax.fori_loop(0, tm, issue, 0)
    lax.fori_loop(0, tm, wait, 0)
    y = jnp.concatenate([buf[pl.ds(s, tm, stride=slab), :] for s in range(slab)], axis=1)
    o_ref[...] = x_ref[...] + g_ref[...] * y


def _collect(grp, l, pos, ys, x, tm):
    rows, d = x.shape
    slab = d // LANE
    return _call(
        functools.partial(_collect_kernel, slab=slab), "moe_collect", (rows // tm, 1),
        [pl.BlockSpec(memory_space=pl.ANY),
         pl.BlockSpec((tm, d), lambda i, j, pos: (i, 0)),
         grp.mod_spec(l, 5, d, tm=tm)],
        pl.BlockSpec((tm, d), lambda i, j, pos: (i, 0)),
        jax.ShapeDtypeStruct((rows, d), F32),
        scratch=[pltpu.VMEM((tm * slab, LANE), F32), pltpu.SemaphoreType.DMA(())],
        sem=("arbitrary", "arbitrary"), n_prefetch=1,
    )(pos, ys, x, grp.mods)


def _moe_by_group(grp, l, x, g, rw, rb, w_gate, w_up, w_down):
    rows, d = x.shape
    slab = d // LANE
    tm = min(512, rows)
    ht, info, cnt = _moe_route(grp, l, x, g, rw, rb, tm)
    g_idx = info[:, 0].astype(jnp.int32)
    rank = info[:, 1].astype(jnp.int32)
    counts = cnt[0, :N_GROUPS].astype(jnp.int32)
    tiles = (counts + tm - 1) // tm
    tile_end = jnp.cumsum(tiles)
    pos = (tile_end - tiles)[g_idx] * tm + rank
    n_tiles = rows // tm + N_GROUPS
    tile_group = jnp.sum(jnp.arange(n_tiles)[:, None] >= tile_end[None, :N_GROUPS - 1], axis=1).astype(jnp.int32)
    n_used = tile_end[N_GROUPS - 1:].astype(jnp.int32)
    xs = _dispatch(pos, ht, n_tiles * tm, slab)
    ys = _moe_sorted(l, tile_group, n_used, xs, rw, rb, w_gate, w_up, w_down, tm)
    return _collect(grp, l, pos, ys, x, tm)


def _final_norm_kernel(x_ref, g_ref, o_ref):
    x = x_ref[...]
    o_ref[...] = x * lax.rsqrt(jnp.mean(x * x, axis=-1, keepdims=True) + EPS) * g_ref[...]


def _final_norm(x, g, tm):
    rows, d = x.shape
    return _call(
        _final_norm_kernel, "final_norm", (rows // tm,),
        [pl.BlockSpec((tm, d), lambda i: (i, 0)), pl.BlockSpec((1, d), lambda i: (0, 0))],
        pl.BlockSpec((tm, d), lambda i: (i, 0)),
        jax.ShapeDtypeStruct((rows, d), F32), sem=("parallel",),
    )(x, g.reshape(1, d))


def kernel(x_prompt, x_sample, cache_k, cache_v, state_conv, state_lru, page_table, c_prompt, c_sample, ada_w, ada_b, norm_mix_g, norm_ffn_g, norm_final_g, w_in, sb_logit_bias, cm_norm_g, cm_w_s, cm_b_s, lru_conv_w, lru_conv_b, lru_w_a, lru_b_a, lru_w_x, lru_b_x, lru_lambda, w_branch, w_out, router_group_w, router_group_b, router_expert_w, router_expert_b, moe_w_gate, moe_w_up, moe_w_down):
    n_bp, t_p, d = x_prompt.shape
    n_bs, t_s, _ = x_sample.shape
    depth = w_in.shape[0]
    n_heads, dh = cache_k.shape[3], cache_k.shape[4]
    sbw = n_heads * dh
    cmw = cm_norm_g.shape[-1]
    lw = lru_lambda.shape[-1]
    chunk = cm_w_s.shape[-1]
    assert t_s >= CONV_W - 1 and t_s <= SUBLANE and t_p % chunk == 0
    col_q, col_k, col_v = 0, sbw, 2 * sbw
    col_cu, col_cv = 3 * sbw, 3 * sbw + cmw
    col_lx, col_lg = 3 * sbw + 2 * cmw, 3 * sbw + 2 * cmw + lw
    col_gates = 3 * sbw + 2 * cmw + 2 * lw
    assert cmw == sbw and lw == sbw

    n_c = n_bp + n_bs
    c_rows = -(-n_c // SUBLANE) * SUBLANE
    c_all = jnp.pad(jnp.concatenate([c_prompt, c_sample], axis=0), ((0, c_rows - n_c), (0, 0)))
    mods = _ada_mods(c_all, ada_w, ada_b)
    rows_p, rows_s = n_bp * t_p, n_bs * t_s
    tm_p = min(1024, t_p)
    grp_p = _Group(mods[:, :n_bp].reshape(depth, n_bp, 1, N_MODS * d), rows_p, tm_p, t_p, False, False)
    grp_s = _Group(jnp.repeat(mods[:, n_bp:n_c], t_s, axis=1), rows_s, rows_s, t_s, True, True)

    norm_mix = norm_mix_g.reshape(depth, 1, d)
    norm_ffn = norm_ffn_g.reshape(depth, 1, d)
    rw = jnp.pad(jnp.concatenate([router_group_w, router_expert_w], axis=-1),
                 ((0, 0), (0, 0), (0, LANE - N_GROUPS - N_EXPERTS)))
    rb = jnp.pad(jnp.concatenate([router_group_b, router_expert_b], axis=-1),
                 ((0, 0), (0, LANE - N_GROUPS - N_EXPERTS))).reshape(depth, 1, LANE)

    n_rep = rows_s // t_s
    cm_w_small = jnp.tile(cm_w_s[:, :, :t_s, :t_s], (1, 1, n_rep, n_rep))
    cm_b_small = jnp.tile(cm_b_s[:, :, :t_s], (1, 1, n_rep))[..., None]
    cm_b_full = cm_b_s[..., None]
    zero_hist = jnp.zeros((n_bp, SUBLANE, lw), F32)
    zero_h = jnp.zeros((n_bp, lw), F32)
    hist_s = jnp.pad(state_conv, ((0, 0), (0, 0), (SUBLANE - (CONV_W - 1), 0), (0, 0)))
    tt_p = min(256, t_p)
    tm_cm = min(512, t_p)

    def layer(l, grp, x, prompt):
        outs = _in_proj(grp, l, _norm_mod(grp, l, x, norm_mix), w_in, col_k, 2 * sbw)
        proj = outs[0]
        blk = lambda c: c // sbw
        if prompt:
            k_new = outs[1][:, :sbw].reshape(n_bp, t_p, n_heads, dh)
            v_new = outs[1][:, sbw:].reshape(n_bp, t_p, n_heads, dh)
            y_a = _sb_attention_prompt(proj, sb_logit_bias[l], n_bp, t_p, n_heads, dh)
            y_b, = _chunk_mix(proj, blk(col_cu), blk(col_cv), cm_norm_g[l], cm_w_s[l], cm_b_full[l],
                              tm_cm, chunk, chunk, False, False)
            vn = None
            y_c, tail, hs = _rglru(proj, blk(col_lx), blk(col_lg), n_bp, t_p, tt_p, zero_hist, zero_h,
                                   lru_conv_w[l], lru_conv_b[l], lru_w_a[l], lru_b_a[l], lru_w_x[l], lru_b_x[l],
                                   lru_lambda[l], False)
            t_last = t_p
        else:
            q3 = proj[:, col_q:col_q + sbw].reshape(n_bs, t_s, sbw)
            k3 = proj[:, col_k:col_k + sbw].reshape(n_bs, t_s, sbw)
            v3 = proj[:, col_v:col_v + sbw].reshape(n_bs, t_s, sbw)
            k_new = k3.reshape(n_bs, t_s, n_heads, dh)
            v_new = v3.reshape(n_bs, t_s, n_heads, dh)
            y_a = _sb_attention_sample(l, q3, k3, v3, cache_k, cache_v, page_table, sb_logit_bias[l])
            y_a = y_a.reshape(rows_s, sbw)
            y_b, vn = _chunk_mix(proj, blk(col_cu), blk(col_cv), cm_norm_g[l], cm_w_small[l], cm_b_small[l],
                                 rows_s, rows_s, t_s, True, True)
            vn = vn.reshape(n_bs, t_s, cmw)
            pad_t = lambda a: jnp.pad(a.reshape(n_bs, t_s, lw), ((0, 0), (0, SUBLANE - t_s), (0, 0))).reshape(
                n_bs * SUBLANE, lw)
            xg = jnp.concatenate([pad_t(proj[:, col_lx:col_lx + lw]), pad_t(proj[:, col_lg:col_lg + lw])], axis=1)
            y_c, tail, hs = _rglru(xg, 0, 1, n_bs, SUBLANE, SUBLANE, hist_s[l], state_lru[l],
                                   lru_conv_w[l], lru_conv_b[l], lru_w_a[l], lru_b_a[l], lru_w_x[l], lru_b_x[l],
                                   lru_lambda[l], True)
            y_c = y_c.reshape(n_bs, SUBLANE, lw)[:, :t_s].reshape(rows_s, lw)
            t_last = t_s
        end = (t_last - 1) % SUBLANE + 1
        conv_new = tail[:, end - (CONV_W - 1):end]
        h_new = hs[:, end - 1]
        merged = _branch_merge(grp, l, y_a, y_b, y_c, proj, col_gates, w_branch)
        x = _out_proj(grp, l, merged, w_out, x)
        if prompt:
            x = _moe_by_group(grp, l, x, norm_ffn, rw[l], rb[l], moe_w_gate, moe_w_up, moe_w_down)
        else:
            h, route = _moe_prep(grp, l, x, norm_ffn, rw[l], rb[l])
            x = _moe(grp, l, h, route, moe_w_gate, moe_w_up, moe_w_down, x)
        return x, (k_new, v_new, vn, conv_new, h_new)

    def trunk(grp, x, prompt):
        outs = []
        for l in range(depth):
            x, o = layer(l, grp, x, prompt)
            outs.append(o)
        y = _final_norm(x, norm_final_g, grp.tm)
        return y, [None if prompt and i == 2 else jnp.stack([o[i] for o in outs]) for i in range(5)]

    y_p, (k_p, v_p, _, conv_p, lru_p) = trunk(grp_p, x_prompt.reshape(rows_p, d), True)
    y_s, (k_s, v_s, cv_s, conv_s, lru_s) = trunk(grp_s, x_sample.reshape(rows_s, d), False)
    return (y_p.reshape(n_bp, t_p, d), y_s.reshape(n_bs, t_s, d), k_p, v_p, k_s, v_s, cv_s,
            conv_p, conv_s, lru_p, lru_s)
```

```python
import functools

import jax
import jax.numpy as jnp
from jax import lax
from jax.experimental import pallas as pl
from jax.experimental.pallas import tpu as pltpu

F32 = jnp.float32
BF16 = jnp.bfloat16

EPS = 1e-6
LRU_C = 8.0
N_MODS = 6
N_GROUPS = 4
EXPERTS_PER_GROUP = 4
N_EXPERTS = N_GROUPS * EXPERTS_PER_GROUP
CONV_W = 4
N_BRANCH = 3
LANE = 128
SUBLANE = 8
VMEM_LIMIT = 56 * 1024 * 1024


def _call(kernel, name, grid, in_specs, out_specs, out_shape, scratch=(), sem=None, n_prefetch=0):
    grid_spec = pltpu.PrefetchScalarGridSpec(
        num_scalar_prefetch=n_prefetch, grid=grid, in_specs=in_specs, out_specs=out_specs,
        scratch_shapes=list(scratch))
    return pl.pallas_call(
        kernel, grid_spec=grid_spec, out_shape=out_shape, name=name,
        compiler_params=pltpu.CompilerParams(dimension_semantics=sem, vmem_limit_bytes=VMEM_LIMIT))


def _bdot(a, b):
    return jnp.dot(a.astype(BF16), b.astype(BF16), preferred_element_type=F32)


def _split_hi_lo(x):
    hi = x.astype(BF16)
    lo = (x - hi.astype(F32)).astype(BF16)
    return hi, lo


def _hdot(a, b):
    a_hi, a_lo = _split_hi_lo(a.astype(F32))
    b_hi, b_lo = _split_hi_lo(b.astype(F32))
    dot = functools.partial(jnp.dot, preferred_element_type=F32)
    m = a.shape[0]
    if m % (2 * SUBLANE):
        return dot(a_hi, b_hi) + dot(a_lo, b_hi) + dot(a_hi, b_lo)
    both = dot(jnp.concatenate([a_hi, a_lo], axis=0), b_hi)
    return both[:m] + both[m:] + dot(a_hi, b_lo)


def _mm(a, b, precise):
    return _hdot(a, b) if precise else _bdot(a, b)


def _softplus_neg_abs(z):
    u = jnp.exp(-jnp.abs(z))
    return jnp.where(u < 1e-3, u * (1.0 - u * (0.5 - u * (1.0 / 3.0))), jnp.log(1.0 + u))


def _log_sigmoid_pair(z):
    t = _softplus_neg_abs(z)
    return jnp.minimum(z, 0.0) - t, -jnp.maximum(z, 0.0) - t


def _neg_expm1(y):
    small = -y * (1.0 + y * (0.5 + y * (1.0 / 6.0 + y * (1.0 / 24.0))))
    return jnp.where(y > -0.05, small, 1.0 - jnp.exp(y))


def _silu(x):
    return x * jax.nn.sigmoid(x)


def _gelu(x):
    return jax.nn.gelu(x, approximate=True)


def _ada_kernel(c_ref, w_ref, b_ref, o_ref):
    o_ref[...] = _hdot(_silu(c_ref[...]), w_ref[...]) + b_ref[...]


def _ada_mods(c_all, ada_w, ada_b):
    n_l, d, n = ada_w.shape
    r = c_all.shape[0]
    tn = 1024
    return _call(
        _ada_kernel, "ada_mods", (n_l, n // tn),
        [pl.BlockSpec((r, d), lambda l, j: (0, 0)),
         pl.BlockSpec((None, d, tn), lambda l, j: (l, 0, j)),
         pl.BlockSpec((None, 1, tn), lambda l, j: (l, 0, j))],
        pl.BlockSpec((None, r, tn), lambda l, j: (l, 0, j)),
        jax.ShapeDtypeStruct((n_l, r, n), F32), sem=("parallel", "parallel"),
    )(c_all, ada_w, ada_b.reshape(n_l, 1, n))


class _Group:
    def __init__(self, mods, rows, tm, rows_per_batch, per_token, precise):
        self.mods, self.rows, self.tm = mods, rows, tm
        self.rows_per_batch, self.per_token = rows_per_batch, per_token
        self.precise = precise
        self.act = F32 if precise else BF16

    def mod_spec(self, l, k, d, tn=None, tm=None):
        tn_ = d if tn is None else tn
        tm_ = self.tm if tm is None else tm
        nj = d // tn_

        def col(j):
            return k * nj + (0 if tn is None else j)

        if self.per_token:
            return pl.BlockSpec((None, tm_, tn_), lambda i, j, *_: (l, i, col(j)))
        tpb = self.rows_per_batch // tm_
        return pl.BlockSpec((None, None, 1, tn_), lambda i, j, *_: (l, i // tpb, 0, col(j)))


def _modulated_norm(x, g, sc, sh):
    y = x * lax.rsqrt(jnp.mean(x * x, axis=-1, keepdims=True) + EPS) * g
    return y * (1.0 + sc) + sh


def _norm_mod_kernel(x_ref, g_ref, sh_ref, sc_ref, o_ref):
    o_ref[...] = _modulated_norm(x_ref[...], g_ref[...], sc_ref[...], sh_ref[...]).astype(o_ref.dtype)


def _norm_mod(grp, l, x, g):
    rows, d = x.shape
    tm = min(grp.tm, 512)
    return _call(
        _norm_mod_kernel, "norm_mod", (rows // tm, 1),
        [pl.BlockSpec((tm, d), lambda i, j: (i, 0)),
         pl.BlockSpec((None, 1, d), lambda i, j: (l, 0, 0)),
         grp.mod_spec(l, 0, d, tm=tm), grp.mod_spec(l, 1, d, tm=tm)],
        pl.BlockSpec((tm, d), lambda i, j: (i, 0)),
        jax.ShapeDtypeStruct((rows, d), grp.act), sem=("parallel", "arbitrary"),
    )(x, g, grp.mods, grp.mods)


def _in_proj_kernel(h_ref, w_ref, o_ref, *kv_ref, precise, kv_lo, kv_hi):
    y = _mm(h_ref[...], w_ref[...], precise)
    o_ref[...] = y.astype(o_ref.dtype)
    if kv_ref:
        j = pl.program_id(1)

        @pl.when((j >= kv_lo) & (j < kv_hi))
        def _():
            kv_ref[0][...] = y


def _in_proj(grp, l, h, w_in, kv_col0, kv_cols):
    rows, d = h.shape
    n = w_in.shape[2]
    tm = min(2 * grp.tm, grp.rows_per_batch) if not grp.per_token else grp.tm
    tn = 512
    kv_lo, kv_hi = kv_col0 // tn, (kv_col0 + kv_cols) // tn
    out_specs = [pl.BlockSpec((tm, tn), lambda i, j: (i, j))]
    out_shape = [jax.ShapeDtypeStruct((rows, n), grp.act)]
    if not grp.precise:
        out_specs.append(pl.BlockSpec((tm, tn), lambda i, j: (i, jnp.clip(j - kv_lo, 0, kv_hi - kv_lo - 1))))
        out_shape.append(jax.ShapeDtypeStruct((rows, kv_cols), F32))
    return _call(
        functools.partial(_in_proj_kernel, precise=grp.precise, kv_lo=kv_lo, kv_hi=kv_hi), "in_proj",
        (rows // tm, n // tn),
        [pl.BlockSpec((tm, d), lambda i, j: (i, 0)),
         pl.BlockSpec((None, d, tn), lambda i, j: (l, 0, j))],
        out_specs, out_shape, sem=("parallel", "arbitrary"),
    )(h, w_in)


def _cumsum_matrix(n):
    j = jnp.arange(n)[:, None]
    s = jnp.arange(n)[None, :]
    return jnp.concatenate([(j > s).astype(BF16), jnp.ones((n, n), BF16)], axis=1)


def _log_sigmoid_pair_fast(z):
    log_beta = jnp.minimum(z, 0.0) - jnp.log(1.0 + jnp.exp(-jnp.abs(z)))
    return log_beta, log_beta - z


def _sb_prompt_kernel(bias_ref, q_ref, k_ref, v_ref, u_ref, o_ref, c_ref, acc_ref, *, tq, ck, scale):
    h = pl.program_id(1)
    qi = pl.program_id(2)
    bias = bias_ref[h]
    q = q_ref[...].astype(BF16)
    acc_ref[...] = jnp.zeros_like(acc_ref)
    c_ref[...] = jnp.zeros_like(c_ref)
    n_chunks = tq // ck
    contract_last = (((1,), (1,)), ((), ()))

    for ci in reversed(range(n_chunks)):
        r0 = ci * ck
        start = pl.multiple_of(qi * tq + r0, ck)
        kc = k_ref[pl.ds(start, ck), :].astype(BF16)
        vc = v_ref[pl.ds(start, ck), :].astype(BF16)
        z = lax.dot_general(q[r0:, :], kc, contract_last, preferred_element_type=F32) * scale + bias
        log_beta, log_keep = _log_sigmoid_pair_fast(z)
        t_pos = lax.broadcasted_iota(jnp.int32, z.shape, 0)
        s_pos = lax.broadcasted_iota(jnp.int32, z.shape, 1)
        mask = s_pos < t_pos
        log_keep = jnp.where(mask, log_keep, 0.0)
        cs = jnp.dot(log_keep.astype(BF16), u_ref[...], preferred_element_type=F32)
        c = c_ref[r0:, :]
        p = jnp.where(mask, jnp.exp(log_beta + cs[:, :ck] + c), 0.0)
        acc_ref[r0:, :] += jnp.dot(p.astype(BF16), vc, preferred_element_type=F32)
        c_ref[r0:, :] = c + cs[:, ck:]

    def body(i, carry):
        start = pl.multiple_of((qi - 1 - i) * tq, tq)
        kt = k_ref[pl.ds(start, tq), :].astype(BF16)
        vt = v_ref[pl.ds(start, tq), :].astype(BF16)
        z = lax.dot_general(q, kt, contract_last, preferred_element_type=F32) * scale + bias
        log_beta, log_keep = _log_sigmoid_pair_fast(z)
        lk = log_keep.astype(BF16)
        c = c_ref[...]
        ps = [None] * n_chunks
        for ci in reversed(range(n_chunks)):
            cols = slice(ci * ck, (ci + 1) * ck)
            cs = jnp.dot(lk[:, cols], u_ref[...], preferred_element_type=F32)
            ps[ci] = jnp.exp(log_beta[:, cols] + cs[:, :ck] + c).astype(BF16)
            c = c + cs[:, ck:]
        acc_ref[...] += jnp.dot(jnp.concatenate(ps, axis=1), vt, preferred_element_type=F32)
        c_ref[...] = c
        return carry

    lax.fori_loop(0, qi, body, 0)
    o_ref[...] = acc_ref[...].astype(o_ref.dtype)


def _sb_attention_prompt(proj, bias, n_batch, t, n_heads, dh):
    tq = min(1024, t)
    ck = LANE
    nq = t // tq
    kernel = functools.partial(_sb_prompt_kernel, tq=tq, ck=ck, scale=dh ** -0.5)
    return _call(
        kernel, "sb_prompt", (n_batch, n_heads, nq),
        [pl.BlockSpec(memory_space=pltpu.SMEM),
         pl.BlockSpec((tq, dh), lambda b, h, i: (b * nq + i, h)),
         pl.BlockSpec((t, dh), lambda b, h, i: (b, n_heads + h)),
         pl.BlockSpec((t, dh), lambda b, h, i: (b, 2 * n_heads + h)),
         pl.BlockSpec((ck, 2 * ck), lambda b, h, i: (0, 0))],
        pl.BlockSpec((tq, dh), lambda b, h, i: (b * nq + i, h)),
        jax.ShapeDtypeStruct((n_batch * t, n_heads * dh), BF16),
        scratch=[pltpu.VMEM((tq, ck), F32), pltpu.VMEM((tq, dh), F32)],
        sem=("parallel", "parallel", "arbitrary"),
    )(bias, proj, proj, proj, _cumsum_matrix(ck))


def _sb_sample_kernel(pt_ref, qbd_ref, bias_ref, kn_ref, vn_ref, ut_ref, *rest, n_pp, page, n_heads, s_new, nhq,
                      scale):
    k_refs = rest[:n_pp]
    v_refs = rest[n_pp:2 * n_pp]
    o_ref, c_ref, acc_ref = rest[2 * n_pp:]
    j = pl.program_id(1)
    bias = bias_ref[...]

    def rows_by_key(ref):
        return jnp.concatenate(
            [ref[pl.ds(h, page, stride=n_heads), :].astype(BF16) for h in range(n_heads)], axis=1)

    def visit(k_list, v_list, masked):
        n = len(k_list)
        k = jnp.concatenate([rows_by_key(r) for r in k_list], axis=0)
        v = jnp.concatenate([rows_by_key(r) for r in v_list], axis=0)
        z = jnp.dot(k, qbd_ref[...], preferred_element_type=F32) * scale + bias
        log_beta, log_keep = _log_sigmoid_pair(z)
        if masked:
            s_pos = lax.broadcasted_iota(jnp.int32, z.shape, 0)
            q_pos = lax.broadcasted_iota(jnp.int32, z.shape, 1) % s_new
            mask = s_pos < q_pos
            log_keep = jnp.where(mask, log_keep, 0.0)
        hi, lo = _split_hi_lo(log_keep)
        c = c_ref[...]
        ps = [None] * n
        for i in reversed(range(n)):
            rows = slice(i * page, (i + 1) * page)
            cs = jnp.dot(ut_ref[...], jnp.concatenate([hi[rows], lo[rows]], axis=0), preferred_element_type=F32)
            ps[i] = jnp.exp(log_beta[rows] + cs + c)
            c = c + cs[0:1, :] + log_keep[i * page:i * page + 1, :]
        c_ref[...] = c
        p = ps[0] if n == 1 else jnp.concatenate(ps, axis=0)
        if masked:
            p = jnp.where(mask, p, 0.0)
        pt = jnp.transpose(p)[:nhq, :].astype(BF16)
        acc_ref[...] += jnp.dot(pt, v, preferred_element_type=F32)

    @pl.when(j == 0)
    def _():
        c_ref[...] = jnp.zeros_like(c_ref)
        acc_ref[...] = jnp.zeros_like(acc_ref)
        visit([kn_ref], [vn_ref], True)

    visit(list(k_refs)[::-1], list(v_refs)[::-1], False)

    @pl.when(j == pl.num_programs(1) - 1)
    def _():
        o_ref[...] = acc_ref[...]


def _sb_attention_sample(l, q, k_new, v_new, cache_k, cache_v, page_table, bias):
    n_b, s_new, width = q.shape
    n_l, n_phys, page, n_heads, dh = cache_k.shape
    n_pages = page_table.shape[1]
    nhq = n_heads * s_new
    n_pp = 8 if n_pages % 8 == 0 else 1
    ck = cache_k.reshape(n_l, n_phys, page * n_heads, dh)
    cv = cache_v.reshape(n_l, n_phys, page * n_heads, dh)
    q4 = q.reshape(n_b, s_new, n_heads, dh)
    qbd = jnp.einsum('bqhd,hg->bhdgq', q4, jnp.eye(n_heads, dtype=q.dtype)).reshape(n_b, width, nhq)
    qbd = jnp.pad(qbd, ((0, 0), (0, 0), (0, LANE - nhq))).astype(BF16)
    bias_l = jnp.pad(jnp.repeat(bias, s_new), (0, LANE - nhq)).reshape(1, LANE)
    kn = jnp.pad(k_new, ((0, 0), (0, page - s_new), (0, 0))).reshape(n_b, page * n_heads, dh)
    vn = jnp.pad(v_new, ((0, 0), (0, page - s_new), (0, 0))).reshape(n_b, page * n_heads, dh)
    jj = jnp.arange(page)[None, :]
    ss = jnp.arange(page)[:, None]
    ut = (jj > ss).astype(BF16)
    ut2 = jnp.concatenate([ut, ut], axis=1)

    def page_spec(i):
        return pl.BlockSpec((None, None, page * n_heads, dh),
                            lambda b, j, pt: (l, pt[b, n_pages - 1 - (j * n_pp + i)], 0, 0))

    kernel = functools.partial(_sb_sample_kernel, n_pp=n_pp, page=page, n_heads=n_heads, s_new=s_new, nhq=nhq,
                               scale=dh ** -0.5)
    out = _call(
        kernel, "sb_sample", (n_b, n_pages // n_pp),
        [pl.BlockSpec((None, width, LANE), lambda b, j, pt: (b, 0, 0)),
         pl.BlockSpec((1, LANE), lambda b, j, pt: (0, 0)),
         pl.BlockSpec((None, page * n_heads, dh), lambda b, j, pt: (b, 0, 0)),
         pl.BlockSpec((None, page * n_heads, dh), lambda b, j, pt: (b, 0, 0)),
         pl.BlockSpec((page, 2 * page), lambda b, j, pt: (0, 0))]
        + [page_spec(i) for i in range(n_pp)] + [page_spec(i) for i in range(n_pp)],
        pl.BlockSpec((None, nhq, width), lambda b, j, pt: (b, 0, 0)),
        jax.ShapeDtypeStruct((n_b, nhq, width), F32),
        scratch=[pltpu.VMEM((1, LANE), F32), pltpu.VMEM((nhq, width), F32)],
        sem=("parallel", "arbitrary"), n_prefetch=1,
    )(page_table, qbd, bias_l, kn, vn, ut2, *([ck] * n_pp), *([cv] * n_pp))
    o5 = out.reshape(n_b, n_heads, s_new, n_heads, dh)
    hh = jnp.arange(n_heads)
    diag = o5[:, hh, :, hh, :]
    return jnp.transpose(diag, (1, 2, 0, 3)).reshape(n_b, s_new, width)


def _chunk_mix_kernel(u_ref, v_ref, g_ref, w_ref, b_ref, y_ref, *vn_out, lc, period, groups, precise):
    u = _gelu(u_ref[...].astype(F32))
    v = _gelu(v_ref[...].astype(F32))
    mu = jnp.mean(v, axis=-1, keepdims=True)
    var = jnp.mean(jnp.square(v - mu), axis=-1, keepdims=True)
    vn = (v - mu) * lax.rsqrt(var + EPS) * g_ref[...]
    if vn_out:
        vn_out[0][...] = vn
    vb = vn if precise else vn.astype(BF16)
    tm, width = u.shape
    gd = width // groups
    n_ch = tm // lc
    r = lax.broadcasted_iota(jnp.int32, (lc, lc), 0)
    c = lax.broadcasted_iota(jnp.int32, (lc, lc), 1)
    mask = c <= r if period == lc else (r // period == c // period) & (c <= r)
    for g in range(groups):
        w = jnp.where(mask, w_ref[g], 0.0)
        rhs = jnp.concatenate([vb[ch * lc:(ch + 1) * lc, g * gd:(g + 1) * gd] for ch in range(n_ch)], axis=1)
        mixed = _mm(w, rhs, precise) + b_ref[g]
        for ch in range(n_ch):
            rows = slice(ch * lc, (ch + 1) * lc)
            cols = slice(g * gd, (g + 1) * gd)
            y_ref[rows, cols] = (u[rows, cols] * mixed[:, ch * gd:(ch + 1) * gd]).astype(y_ref.dtype)


def _chunk_mix(proj, col_u, col_v, g, w, b, tm, lc, period, emit_vn, precise):
    rows = proj.shape[0]
    groups = w.shape[0]
    width = g.shape[-1]
    kernel = functools.partial(_chunk_mix_kernel, lc=lc, period=period, groups=groups, precise=precise)
    out_specs = [pl.BlockSpec((tm, width), lambda i: (i, 0))]
    out_shape = [jax.ShapeDtypeStruct((rows, width), F32 if precise else BF16)]
    if emit_vn:
        out_specs.append(pl.BlockSpec((tm, width), lambda i: (i, 0)))
        out_shape.append(jax.ShapeDtypeStruct((rows, width), F32))
    return _call(
        kernel, "chunk_mix", (rows // tm,),
        [pl.BlockSpec((tm, width), lambda i: (i, col_u)),
         pl.BlockSpec((tm, width), lambda i: (i, col_v)),
         pl.BlockSpec((1, width), lambda i: (0, 0)),
         pl.BlockSpec((groups, lc, lc), lambda i: (0, 0, 0)),
         pl.BlockSpec((groups, lc, 1), lambda i: (0, 0, 0))],
        out_specs, out_shape, sem=("parallel",),
    )(proj, proj, g.reshape(1, width), w, b)


def _shift_rows(x, d, fill, row):
    tt = x.shape[0]
    if d % SUBLANE == 0:
        return jnp.concatenate([jnp.full((d,) + x.shape[1:], fill, x.dtype), x[:tt - d]], axis=0)
    return jnp.where(row < d, fill, pltpu.roll(x, d, 0))


def _rglru_kernel(x_ref, gate_ref, hist_ref, h0_ref, cw_ref, cb_ref, wa_ref, ba_ref, wx_ref, bx_ref, lam_ref,
                  y_ref, tail_ref, hs_ref, hist_s, h_s, *, precise):
    t = pl.program_id(1)

    @pl.when(t == 0)
    def _():
        hist_s[...] = hist_ref[...]
        h_s[...] = jnp.broadcast_to(h0_ref[...], h_s.shape)

    x = x_ref[...].astype(F32)
    tt, width = x.shape
    row = lax.broadcasted_iota(jnp.int32, (tt, width), 0)
    row8 = lax.broadcasted_iota(jnp.int32, (SUBLANE, width), 0)
    hist = hist_s[...]
    cw = cw_ref[...]
    xc = cb_ref[...] + x * cw[CONV_W - 1:CONV_W, :]
    for s in range(1, CONV_W):
        xr = pltpu.roll(x, s, 0)
        top = jnp.where(row8 < s, pltpu.roll(hist, s, 0), xr[:SUBLANE])
        xs = top if tt == SUBLANE else jnp.concatenate([top, xr[SUBLANE:]], axis=0)
        xc = xc + xs * cw[CONV_W - 1 - s:CONV_W - s, :]

    n_blk = wa_ref.shape[0]
    bd = width // n_blk
    ra, ri = [], []
    for n in range(n_blk):
        xb = xc[:, n * bd:(n + 1) * bd]
        ra.append(_mm(xb, wa_ref[n], precise))
        ri.append(_mm(xb, wx_ref[n], precise))
    r_gate = jax.nn.sigmoid(jnp.concatenate(ra, axis=1) + ba_ref[...])
    i_gate = jax.nn.sigmoid(jnp.concatenate(ri, axis=1) + bx_ref[...])
    lam = lam_ref[...]
    log_sig_lam = jnp.minimum(lam, 0.0) - _softplus_neg_abs(lam)
    log_a = LRU_C * r_gate * log_sig_lam
    a = jnp.exp(log_a)
    b = jnp.sqrt(_neg_expm1(2.0 * log_a)) * (i_gate * xc)

    d = 1
    while d < tt:
        b = a * _shift_rows(b, d, 0.0, row) + b
        a = a * _shift_rows(a, d, 1.0, row)
        d *= 2
    hs = a * h_s[SUBLANE - 1:SUBLANE, :] + b

    y_ref[...] = (hs * _gelu(gate_ref[...].astype(F32))).astype(y_ref.dtype)
    hist_s[...] = x[tt - SUBLANE:, :]
    h_s[...] = hs[tt - SUBLANE:, :]
    tail_ref[...] = x[tt - SUBLANE:, :]
    hs_ref[...] = hs[tt - SUBLANE:, :]


def _rglru(xsrc, col_x, col_g, n_batch, t, tt, hist8, h0, cw, cb, wa, ba, wx, bx, lam, precise):
    width = lam.shape[-1]
    nt = t // tt
    row2 = lambda a: a.reshape(1, width)
    full = lambda shape: pl.BlockSpec(shape, lambda b, i: (0,) * len(shape))
    return _call(
        functools.partial(_rglru_kernel, precise=precise), "rglru", (n_batch, nt),
        [pl.BlockSpec((tt, width), lambda b, i: (b * nt + i, col_x)),
         pl.BlockSpec((tt, width), lambda b, i: (b * nt + i, col_g)),
         pl.BlockSpec((None, SUBLANE, width), lambda b, i: (b, 0, 0)),
         pl.BlockSpec((None, 1, width), lambda b, i: (b, 0, 0)),
         full((CONV_W, width)), full((1, width)),
         full(wa.shape), full((1, width)), full(wx.shape), full((1, width)), full((1, width))],
        [pl.BlockSpec((tt, width), lambda b, i: (b * nt + i, 0)),
         pl.BlockSpec((None, SUBLANE, width), lambda b, i: (b, 0, 0)),
         pl.BlockSpec((None, SUBLANE, width), lambda b, i: (b, 0, 0))],
        [jax.ShapeDtypeStruct((n_batch * t, width), F32 if precise else BF16),
         jax.ShapeDtypeStruct((n_batch, SUBLANE, width), F32),
         jax.ShapeDtypeStruct((n_batch, SUBLANE, width), F32)],
        scratch=[pltpu.VMEM((SUBLANE, width), F32), pltpu.VMEM((SUBLANE, width), F32)],
        sem=("parallel", "arbitrary"),
    )(xsrc, xsrc, hist8, h0.reshape(n_batch, 1, width), cw, row2(cb), wa, row2(ba), wx, row2(bx), row2(lam))


def _branch_kernel(ya_ref, yb_ref, yc_ref, ga_ref, gb_ref, gc_ref, w_ref, o_ref, *, precise):
    acc = None
    for n, (y_ref, g_ref) in enumerate(((ya_ref, ga_ref), (yb_ref, gb_ref), (yc_ref, gc_ref))):
        term = jax.nn.sigmoid(g_ref[...].astype(F32)) * _mm(y_ref[...], w_ref[n], precise)
        acc = term if acc is None else acc + term
    o_ref[...] = acc.astype(o_ref.dtype)


def _branch_merge(grp, l, ya, yb, yc, proj, gate_col0, w_branch):
    rows, bw = ya.shape
    d = w_branch.shape[-1]
    tm, tn = grp.tm, 512
    y_spec = pl.BlockSpec((tm, bw), lambda i, j: (i, 0))

    def gate_spec(n):
        return pl.BlockSpec((tm, tn), lambda i, j: (i, (gate_col0 + n * d) // tn + j))

    return _call(
        functools.partial(_branch_kernel, precise=grp.precise), "branch_merge", (rows // tm, d // tn),
        [y_spec, y_spec, y_spec, gate_spec(0), gate_spec(1), gate_spec(2),
         pl.BlockSpec((None, N_BRANCH, bw, tn), lambda i, j: (l, 0, 0, j))],
        pl.BlockSpec((tm, tn), lambda i, j: (i, j)),
        jax.ShapeDtypeStruct((rows, d), grp.act), sem=("parallel", "arbitrary"),
    )(ya, yb, yc, proj, proj, proj, w_branch)


def _out_proj_kernel(m_ref, w_ref, x_ref, g_ref, o_ref, *, precise):
    o_ref[...] = x_ref[...] + g_ref[...] * _mm(m_ref[...], w_ref[...], precise)


def _out_proj(grp, l, merged, w_out, x):
    rows, d = x.shape
    tm, tn = grp.tm, 512
    return _call(
        functools.partial(_out_proj_kernel, precise=grp.precise), "out_proj", (rows // tm, d // tn),
        [pl.BlockSpec((tm, d), lambda i, j: (i, 0)),
         pl.BlockSpec((None, d, tn), lambda i, j: (l, 0, j)),
         pl.BlockSpec((tm, tn), lambda i, j: (i, j)),
         grp.mod_spec(l, 2, d, tn)],
        pl.BlockSpec((tm, tn), lambda i, j: (i, j)),
        jax.ShapeDtypeStruct((rows, d), F32), sem=("parallel", "arbitrary"),
    )(merged, w_out, x, grp.mods)


def _first_index_of_max(vals):
    m = functools.reduce(jnp.maximum, vals)
    idx = jnp.full(m.shape, len(vals) - 1, jnp.int32)
    for k in reversed(range(len(vals) - 1)):
        idx = jnp.where(vals[k] == m, k, idx)
    return m, idx


def _moe_prep_kernel(x_ref, g_ref, sh_ref, sc_ref, rw_ref, rb_ref, h_ref, r_ref):
    h = _modulated_norm(x_ref[...], g_ref[...], sc_ref[...], sh_ref[...])
    h_ref[...] = h.astype(h_ref.dtype)
    h_hi, h_lo = _split_hi_lo(h)
    w_hi, w_lo = _split_hi_lo(rw_ref[...])
    dot = functools.partial(jnp.dot, preferred_element_type=F32)
    logits = dot(h_hi, w_hi) + dot(h_lo, w_hi) + dot(h_hi, w_lo) + rb_ref[...]
    col = lambda k: logits[:, k:k + 1]
    gl = [col(k) for k in range(N_GROUPS)]
    g_max, g_idx = _first_index_of_max(gl)
    g_w = 1.0 / functools.reduce(jnp.add, [jnp.exp(v - g_max) for v in gl])
    e_in = []
    for j in range(EXPERTS_PER_GROUP):
        v = col(N_GROUPS + (N_GROUPS - 1) * EXPERTS_PER_GROUP + j)
        for g in reversed(range(N_GROUPS - 1)):
            v = jnp.where(g_idx == g, col(N_GROUPS + g * EXPERTS_PER_GROUP + j), v)
        e_in.append(v)
    m1, i1 = _first_index_of_max(e_in)
    rest = [jnp.where(i1 == j, -jnp.inf, e_in[j]) for j in range(EXPERTS_PER_GROUP)]
    m2, i2 = _first_index_of_max(rest)
    ratio = jnp.exp(m2 - m1)
    p1 = g_w / (1.0 + ratio)
    p2 = p1 * ratio
    lane = lax.broadcasted_iota(jnp.int32, r_ref.shape, 1)
    e_grp = lane // EXPERTS_PER_GROUP
    e_loc = lane % EXPERTS_PER_GROUP
    w = jnp.where(e_loc == i1, p1, 0.0) + jnp.where(e_loc == i2, p2, 0.0)
    r_ref[...] = jnp.where(e_grp == g_idx, w, 0.0)


def _moe_prep(grp, l, x, g, rw, rb):
    rows, d = x.shape
    tm = grp.tm
    return _call(
        _moe_prep_kernel, "moe_prep", (rows // tm, 1),
        [pl.BlockSpec((tm, d), lambda i, j: (i, 0)),
         pl.BlockSpec((None, 1, d), lambda i, j: (l, 0, 0)),
         grp.mod_spec(l, 3, d), grp.mod_spec(l, 4, d),
         pl.BlockSpec((d, LANE), lambda i, j: (0, 0)),
         pl.BlockSpec((1, LANE), lambda i, j: (0, 0))],
        [pl.BlockSpec((tm, d), lambda i, j: (i, 0)), pl.BlockSpec((tm, LANE), lambda i, j: (i, 0))],
        [jax.ShapeDtypeStruct((rows, d), grp.act), jax.ShapeDtypeStruct((rows, LANE), F32)],
        sem=("parallel", "arbitrary"),
    )(x, g, grp.mods, grp.mods, rw, rb)


def _moe_kernel(h_ref, r_ref, wg_ref, wu_ref, wd_ref, x_ref, g_ref, o_ref, *, precise):
    e = pl.program_id(1)

    @pl.when(e == 0)
    def _():
        o_ref[...] = jnp.zeros_like(o_ref)

    h = h_ref[...]
    a = _mm(h, wg_ref[...], precise)
    b = _mm(h, wu_ref[...], precise)
    lane = lax.broadcasted_iota(jnp.int32, r_ref.shape, 1)
    c = jnp.sum(jnp.where(lane == e, r_ref[...], 0.0), axis=1, keepdims=True)
    o_ref[...] += _mm(_silu(a) * b * c, wd_ref[...], precise)

    @pl.when(e == pl.num_programs(1) - 1)
    def _():
        o_ref[...] = x_ref[...] + g_ref[...] * o_ref[...]


def _moe(grp, l, h, route, w_gate, w_up, w_down, x):
    rows, d = x.shape
    n_e, _, f = w_gate.shape[1:]
    tm = min(grp.tm, 512)
    return _call(
        functools.partial(_moe_kernel, precise=grp.precise), "moe", (rows // tm, n_e),
        [pl.BlockSpec((tm, d), lambda i, e: (i, 0)),
         pl.BlockSpec((tm, LANE), lambda i, e: (i, 0)),
         pl.BlockSpec((None, None, d, f), lambda i, e: (l, e, 0, 0)),
         pl.BlockSpec((None, None, d, f), lambda i, e: (l, e, 0, 0)),
         pl.BlockSpec((None, None, f, d), lambda i, e: (l, e, 0, 0)),
         pl.BlockSpec((tm, d), lambda i, e: (i, 0)),
         grp.mod_spec(l, 5, d, tm=tm)],
        pl.BlockSpec((tm, d), lambda i, e: (i, 0)),
        jax.ShapeDtypeStruct((rows, d), F32), sem=("parallel", "arbitrary"),
    )(h, route, w_gate, w_up, w_down, x, grp.mods)


def _router_logits(h, rw, rb):
    h_hi, h_lo = _split_hi_lo(h)
    w_hi, w_lo = _split_hi_lo(rw)
    dot = functools.partial(jnp.dot, preferred_element_type=F32)
    return dot(h_hi, w_hi) + dot(h_lo, w_hi) + dot(h_hi, w_lo) + rb


def _top2_weights(e_in, g_w):
    m1, i1 = _first_index_of_max(e_in)
    rest = [jnp.where(i1 == j, -jnp.inf, e_in[j]) for j in range(len(e_in))]
    m2, i2 = _first_index_of_max(rest)
    ratio = jnp.exp(m2 - m1)
    p1 = g_w / (1.0 + ratio)
    return i1, p1, i2, p1 * ratio


def _moe_route_kernel(x_ref, g_ref, sh_ref, sc_ref, rw_ref, rb_ref, tri_ref, ht_ref, info_ref, cnt_ref, base_ref,
                      *, slab):
    @pl.when(pl.program_id(0) == 0)
    def _():
        base_ref[...] = jnp.zeros_like(base_ref)

    h = _modulated_norm(x_ref[...], g_ref[...], sc_ref[...], sh_ref[...])
    tm = h.shape[0]
    for s in range(slab):
        ht_ref[pl.ds(s, tm, stride=slab), :] = h[:, s * LANE:(s + 1) * LANE]
    logits = _router_logits(h, rw_ref[...], rb_ref[...])
    _, g_idx = _first_index_of_max([logits[:, k:k + 1] for k in range(N_GROUPS)])
    lane = lax.broadcasted_iota(jnp.int32, (tm, LANE), 1)
    onehot = jnp.where(lane == g_idx, 1.0, 0.0)
    before = jnp.dot(tri_ref[...], onehot.astype(BF16), preferred_element_type=F32)
    base = base_ref[0:1, :]
    rank = jnp.sum(jnp.where(lane == g_idx, before + base, 0.0), axis=1, keepdims=True)
    info_ref[...] = jnp.where(lane == 0, g_idx.astype(F32), jnp.where(lane == 1, rank, 0.0))
    total = base + before[tm - 1:tm, :] + onehot[tm - 1:tm, :]
    base_ref[...] = jnp.broadcast_to(total, base_ref.shape)
    cnt_ref[...] = jnp.broadcast_to(total, cnt_ref.shape)


def _moe_route(grp, l, x, g, rw, rb, tm):
    rows, d = x.shape
    slab = d // LANE
    r = jnp.arange(tm)
    tri = (r[:, None] > r[None, :]).astype(BF16)
    return _call(
        functools.partial(_moe_route_kernel, slab=slab), "moe_route", (rows // tm, 1),
        [pl.BlockSpec((tm, d), lambda i, j: (i, 0)),
         pl.BlockSpec((None, 1, d), lambda i, j: (l, 0, 0)),
         grp.mod_spec(l, 3, d, tm=tm), grp.mod_spec(l, 4, d, tm=tm),
         pl.BlockSpec((d, LANE), lambda i, j: (0, 0)),
         pl.BlockSpec((1, LANE), lambda i, j: (0, 0)),
         pl.BlockSpec((tm, tm), lambda i, j: (0, 0))],
        [pl.BlockSpec((tm * slab, LANE), lambda i, j: (i, 0)),
         pl.BlockSpec((tm, LANE), lambda i, j: (i, 0)),
         pl.BlockSpec((SUBLANE, LANE), lambda i, j: (0, 0))],
        [jax.ShapeDtypeStruct((rows * slab, LANE), F32), jax.ShapeDtypeStruct((rows, LANE), F32),
         jax.ShapeDtypeStruct((SUBLANE, LANE), F32)],
        scratch=[pltpu.VMEM((SUBLANE, LANE), F32)], sem=("arbitrary", "arbitrary"),
    )(x, g, grp.mods, grp.mods, rw, rb, tri)


def _token_copy(src_ref, src_tok, dst_ref, dst_tok, sem, slab):
    return pltpu.make_async_copy(
        src_ref.at[pl.ds(pl.multiple_of(src_tok * slab, slab), slab), :],
        dst_ref.at[pl.ds(pl.multiple_of(dst_tok * slab, slab), slab), :], sem)


def _dispatch_kernel(pos_ref, src_ref, dst_in_ref, dst_ref, sem, *, tb, slab):
    del dst_in_ref
    t0 = pl.program_id(0) * tb

    def issue(t, carry):
        _token_copy(src_ref, t, dst_ref, pos_ref[t0 + t], sem, slab).start()
        return carry

    def wait(t, carry):
        _token_copy(src_ref, 0, dst_ref, 0, sem, slab).wait()
        return carry

    lax.fori_loop(0, tb, issue, 0)
    lax.fori_loop(0, tb, wait, 0)


def _dispatch(pos, ht, n_dst_rows, slab):
    rows = pos.shape[0]
    tb = min(512, rows)
    grid_spec = pltpu.PrefetchScalarGridSpec(
        num_scalar_prefetch=1, grid=(rows // tb,),
        in_specs=[pl.BlockSpec((tb * slab, LANE), lambda i, pos: (i, 0)), pl.BlockSpec(memory_space=pl.ANY)],
        out_specs=pl.BlockSpec(memory_space=pl.ANY),
        scratch_shapes=[pltpu.SemaphoreType.DMA(())])
    return pl.pallas_call(
        functools.partial(_dispatch_kernel, tb=tb, slab=slab), grid_spec=grid_spec, name="moe_dispatch",
        out_shape=jax.ShapeDtypeStruct((n_dst_rows * slab, LANE), F32), input_output_aliases={2: 0},
        compiler_params=pltpu.CompilerParams(dimension_semantics=("arbitrary",), vmem_limit_bytes=VMEM_LIMIT),
    )(pos, ht, jnp.zeros((n_dst_rows * slab, LANE), F32))


def _moe_sorted_kernel(tg_ref, nu_ref, xs_ref, rw_ref, rb_ref, wg_ref, wu_ref, wd_ref, o_ref, h_s, comb_s, acc_s,
                       *, slab):
    i = pl.program_id(0)
    j = pl.program_id(1)
    last = pl.num_programs(1) - 1
    tm = h_s.shape[0]
    used = i < nu_ref[0]
    lane = lax.broadcasted_iota(jnp.int32, (tm, LANE), 1)

    @pl.when(used & (j == 0))
    def _():
        h = jnp.concatenate([xs_ref[pl.ds(s, tm, stride=slab), :] for s in range(slab)], axis=1)
        h_s[...] = h.astype(BF16)
        logits = _router_logits(h, rw_ref[...], rb_ref[...])
        g = tg_ref[i]
        pick = lambda k: jnp.sum(jnp.where(lane == k, logits, 0.0), axis=1, keepdims=True)
        gl = [logits[:, k:k + 1] for k in range(N_GROUPS)]
        g_max = functools.reduce(jnp.maximum, gl)
        g_w = jnp.exp(pick(g) - g_max) / functools.reduce(jnp.add, [jnp.exp(v - g_max) for v in gl])
        e_in = [pick(N_GROUPS + g * EXPERTS_PER_GROUP + k) for k in range(EXPERTS_PER_GROUP)]
        i1, p1, i2, p2 = _top2_weights(e_in, g_w)
        comb_s[...] = jnp.where(lane == i1, p1, 0.0) + jnp.where(lane == i2, p2, 0.0)
        acc_s[...] = jnp.zeros_like(acc_s)

    @pl.when(used)
    def _():
        h = h_s[...]
        a = jnp.dot(h, wg_ref[...].astype(BF16), preferred_element_type=F32)
        b = jnp.dot(h, wu_ref[...].astype(BF16), preferred_element_type=F32)
        c = jnp.sum(jnp.where(lane == j, comb_s[...], 0.0), axis=1, keepdims=True)
        acc_s[...] += _bdot(_silu(a) * b * c, wd_ref[...])

    @pl.when(used & (j == last))
    def _():
        y = acc_s[...]
        for s in range(slab):
            o_ref[pl.ds(s, tm, stride=slab), :] = y[:, s * LANE:(s + 1) * LANE]

    @pl.when(jnp.logical_not(used) & (j == last))
    def _():
        o_ref[...] = jnp.zeros_like(o_ref)


def _moe_sorted(l, tile_group, n_used, xs, rw, rb, w_gate, w_up, w_down, tm):
    n_e, d, f = w_gate.shape[1:]
    slab = d // LANE
    n_tiles = xs.shape[0] // (tm * slab)
    epg = EXPERTS_PER_GROUP

    def expert(i, j, tg, nu):
        return jnp.where(i < nu[0], tg[i] * epg + j, tg[nu[0] - 1] * epg + epg - 1)

    w_spec = lambda a, b: pl.BlockSpec((None, None, a, b), lambda i, j, tg, nu: (l, expert(i, j, tg, nu), 0, 0))
    return _call(
        functools.partial(_moe_sorted_kernel, slab=slab), "moe_sorted", (n_tiles, epg),
        [pl.BlockSpec((tm * slab, LANE), lambda i, j, tg, nu: (jnp.minimum(i, nu[0] - 1), 0)),
         pl.BlockSpec((d, LANE), lambda i, j, tg, nu: (0, 0)),
         pl.BlockSpec((1, LANE), lambda i, j, tg, nu: (0, 0)),
         w_spec(d, f), w_spec(d, f), w_spec(f, d)],
        pl.BlockSpec((tm * slab, LANE), lambda i, j, tg, nu: (i, 0)),
        jax.ShapeDtypeStruct(xs.shape, F32),
        scratch=[pltpu.VMEM((tm, d), BF16), pltpu.VMEM((tm, LANE), F32), pltpu.VMEM((tm, d), F32)],
        sem=("arbitrary", "arbitrary"), n_prefetch=2,
    )(tile_group, n_used, xs, rw, rb, w_gate, w_up, w_down)


def _collect_kernel(pos_ref, ys_ref, x_ref, g_ref, o_ref, buf, sem, *, slab):
    tm = x_ref.shape[0]
    t0 = pl.program_id(0) * tm

    def issue(t, carry):
        _token_copy(ys_ref, pos_ref[t0 + t], buf, t, sem, slab).start()
        return carry

    def wait(t, carry):
        _token_copy(ys_ref, 0, buf, 0, sem, slab).wait()
        return carry

    lax.fori_loop(0, tm, issue, 0)
    lax.fori_loop(0, tm, wait, 0)
    y = jnp.concatenate([buf[pl.ds(s, tm, stride=slab), :] for s in range(slab)], axis=1)
    o_ref[...] = x_ref[...] + g_ref[...] * y


def _collect(grp, l, pos, ys, x, tm):
    rows, d = x.shape
    slab = d // LANE
    return _call(
        functools.partial(_collect_kernel, slab=slab), "moe_collect", (rows // tm, 1),
        [pl.BlockSpec(memory_space=pl.ANY),
         pl.BlockSpec((tm, d), lambda i, j, pos: (i, 0)),
         grp.mod_spec(l, 5, d, tm=tm)],
        pl.BlockSpec((tm, d), lambda i, j, pos: (i, 0)),
        jax.ShapeDtypeStruct((rows, d), F32),
        scratch=[pltpu.VMEM((tm * slab, LANE), F32), pltpu.SemaphoreType.DMA(())],
        sem=("arbitrary", "arbitrary"), n_prefetch=1,
    )(pos, ys, x, grp.mods)


def _moe_by_group(grp, l, x, g, rw, rb, w_gate, w_up, w_down):
    rows, d = x.shape
    slab = d // LANE
    tm = min(512, rows)
    ht, info, cnt = _moe_route(grp, l, x, g, rw, rb, tm)
    g_idx = info[:, 0].astype(jnp.int32)
    rank = info[:, 1].astype(jnp.int32)
    counts = cnt[0, :N_GROUPS].astype(jnp.int32)
    tiles = (counts + tm - 1) // tm
    tile_end = jnp.cumsum(tiles)
    pos = (tile_end - tiles)[g_idx] * tm + rank
    n_tiles = rows // tm + N_GROUPS
    tile_group = jnp.sum(jnp.arange(n_tiles)[:, None] >= tile_end[None, :N_GROUPS - 1], axis=1).astype(jnp.int32)
    n_used = tile_end[N_GROUPS - 1:].astype(jnp.int32)
    xs = _dispatch(pos, ht, n_tiles * tm, slab)
    ys = _moe_sorted(l, tile_group, n_used, xs, rw, rb, w_gate, w_up, w_down, tm)
    return _collect(grp, l, pos, ys, x, tm)


def _final_norm_kernel(x_ref, g_ref, o_ref):
    x = x_ref[...]
    o_ref[...] = x * lax.rsqrt(jnp.mean(x * x, axis=-1, keepdims=True) + EPS) * g_ref[...]


def _final_norm(x, g, tm):
    rows, d = x.shape
    return _call(
        _final_norm_kernel, "final_norm", (rows // tm,),
        [pl.BlockSpec((tm, d), lambda i: (i, 0)), pl.BlockSpec((1, d), lambda i: (0, 0))],
        pl.BlockSpec((tm, d), lambda i: (i, 0)),
        jax.ShapeDtypeStruct((rows, d), F32), sem=("parallel",),
    )(x, g.reshape(1, d))


def kernel(x_prompt, x_sample, cache_k, cache_v, state_conv, state_lru, page_table, c_prompt, c_sample, ada_w, ada_b, norm_mix_g, norm_ffn_g, norm_final_g, w_in, sb_logit_bias, cm_norm_g, cm_w_s, cm_b_s, lru_conv_w, lru_conv_b, lru_w_a, lru_b_a, lru_w_x, lru_b_x, lru_lambda, w_branch, w_out, router_group_w, router_group_b, router_expert_w, router_expert_b, moe_w_gate, moe_w_up, moe_w_down):
    n_bp, t_p, d = x_prompt.shape
    n_bs, t_s, _ = x_sample.shape
    depth = w_in.shape[0]
    n_heads, dh = cache_k.shape[3], cache_k.shape[4]
    sbw = n_heads * dh
    cmw = cm_norm_g.shape[-1]
    lw = lru_lambda.shape[-1]
    chunk = cm_w_s.shape[-1]
    assert t_s >= CONV_W - 1 and t_s <= SUBLANE and t_p % chunk == 0
    col_q, col_k, col_v = 0, sbw, 2 * sbw
    col_cu, col_cv = 3 * sbw, 3 * sbw + cmw
    col_lx, col_lg = 3 * sbw + 2 * cmw, 3 * sbw + 2 * cmw + lw
    col_gates = 3 * sbw + 2 * cmw + 2 * lw
    assert cmw == sbw and lw == sbw

    n_c = n_bp + n_bs
    c_rows = -(-n_c // SUBLANE) * SUBLANE
    c_all = jnp.pad(jnp.concatenate([c_prompt, c_sample], axis=0), ((0, c_rows - n_c), (0, 0)))
    mods = _ada_mods(c_all, ada_w, ada_b)
    rows_p, rows_s = n_bp * t_p, n_bs * t_s
    tm_p = min(1024, t_p)
    grp_p = _Group(mods[:, :n_bp].reshape(depth, n_bp, 1, N_MODS * d), rows_p, tm_p, t_p, False, False)
    grp_s = _Group(jnp.repeat(mods[:, n_bp:n_c], t_s, axis=1), rows_s, rows_s, t_s, True, True)

    norm_mix = norm_mix_g.reshape(depth, 1, d)
    norm_ffn = norm_ffn_g.reshape(depth, 1, d)
    rw = jnp.pad(jnp.concatenate([router_group_w, router_expert_w], axis=-1),
                 ((0, 0), (0, 0), (0, LANE - N_GROUPS - N_EXPERTS)))
    rb = jnp.pad(jnp.concatenate([router_group_b, router_expert_b], axis=-1),
                 ((0, 0), (0, LANE - N_GROUPS - N_EXPERTS))).reshape(depth, 1, LANE)

    n_rep = rows_s // t_s
    cm_w_small = jnp.tile(cm_w_s[:, :, :t_s, :t_s], (1, 1, n_rep, n_rep))
    cm_b_small = jnp.tile(cm_b_s[:, :, :t_s], (1, 1, n_rep))[..., None]
    cm_b_full = cm_b_s[..., None]
    zero_hist = jnp.zeros((n_bp, SUBLANE, lw), F32)
    zero_h = jnp.zeros((n_bp, lw), F32)
    hist_s = jnp.pad(state_conv, ((0, 0), (0, 0), (SUBLANE - (CONV_W - 1), 0), (0, 0)))
    tt_p = min(256, t_p)
    tm_cm = min(512, t_p)

    def layer(l, grp, x, prompt):
        outs = _in_proj(grp, l, _norm_mod(grp, l, x, norm_mix), w_in, col_k, 2 * sbw)
        proj = outs[0]
        blk = lambda c: c // sbw
        if prompt:
            k_new = outs[1][:, :sbw].reshape(n_bp, t_p, n_heads, dh)
            v_new = outs[1][:, sbw:].reshape(n_bp, t_p, n_heads, dh)
            y_a = _sb_attention_prompt(proj, sb_logit_bias[l], n_bp, t_p, n_heads, dh)
            y_b, = _chunk_mix(proj, blk(col_cu), blk(col_cv), cm_norm_g[l], cm_w_s[l], cm_b_full[l],
                              tm_cm, chunk, chunk, False, False)
            vn = None
            y_c, tail, hs = _rglru(proj, blk(col_lx), blk(col_lg), n_bp, t_p, tt_p, zero_hist, zero_h,
                                   lru_conv_w[l], lru_conv_b[l], lru_w_a[l], lru_b_a[l], lru_w_x[l], lru_b_x[l],
                                   lru_lambda[l], False)
            t_last = t_p
        else:
            q3 = proj[:, col_q:col_q + sbw].reshape(n_bs, t_s, sbw)
            k3 = proj[:, col_k:col_k + sbw].reshape(n_bs, t_s, sbw)
            v3 = proj[:, col_v:col_v + sbw].reshape(n_bs, t_s, sbw)
            k_new = k3.reshape(n_bs, t_s, n_heads, dh)
            v_new = v3.reshape(n_bs, t_s, n_heads, dh)
            y_a = _sb_attention_sample(l, q3, k3, v3, cache_k, cache_v, page_table, sb_logit_bias[l])
            y_a = y_a.reshape(rows_s, sbw)
            y_b, vn = _chunk_mix(proj, blk(col_cu), blk(col_cv), cm_norm_g[l], cm_w_small[l], cm_b_small[l],
                                 rows_s, rows_s, t_s, True, True)
            vn = vn.reshape(n_bs, t_s, cmw)
            pad_t = lambda a: jnp.pad(a.reshape(n_bs, t_s, lw), ((0, 0), (0, SUBLANE - t_s), (0, 0))).reshape(
                n_bs * SUBLANE, lw)
            xg = jnp.concatenate([pad_t(proj[:, col_lx:col_lx + lw]), pad_t(proj[:, col_lg:col_lg + lw])], axis=1)
            y_c, tail, hs = _rglru(xg, 0, 1, n_bs, SUBLANE, SUBLANE, hist_s[l], state_lru[l],
                                   lru_conv_w[l], lru_conv_b[l], lru_w_a[l], lru_b_a[l], lru_w_x[l], lru_b_x[l],
                                   lru_lambda[l], True)
            y_c = y_c.reshape(n_bs, SUBLANE, lw)[:, :t_s].reshape(rows_s, lw)
            t_last = t_s
        end = (t_last - 1) % SUBLANE + 1
        conv_new = tail[:, end - (CONV_W - 1):end]
        h_new = hs[:, end - 1]
        merged = _branch_merge(grp, l, y_a, y_b, y_c, proj, col_gates, w_branch)
        x = _out_proj(grp, l, merged, w_out, x)
        if prompt:
            x = _moe_by_group(grp, l, x, norm_ffn, rw[l], rb[l], moe_w_gate, moe_w_up, moe_w_down)
        else:
            h, route = _moe_prep(grp, l, x, norm_ffn, rw[l], rb[l])
            x = _moe(grp, l, h, route, moe_w_gate, moe_w_up, moe_w_down, x)
        return x, (k_new, v_new, vn, conv_new, h_new)

    def trunk(grp, x, prompt):
        outs = []
        for l in range(depth):
            x, o = layer(l, grp, x, prompt)
            outs.append(o)
        y = _final_norm(x, norm_final_g, grp.tm)
        return y, [None if prompt and i == 2 else jnp.stack([o[i] for o in outs]) for i in range(5)]

    y_p, (k_p, v_p, _, conv_p, lru_p) = trunk(grp_p, x_prompt.reshape(rows_p, d), True)
    y_s, (k_s, v_s, cv_s, conv_s, lru_s) = trunk(grp_s, x_sample.reshape(rows_s, d), False)
    return (y_p.reshape(n_bp, t_p, d), y_s.reshape(n_bs, t_s, d), k_p, v_p, k_s, v_s, cv_s,
            conv_p, conv_s, lru_p, lru_s)
```
